```python
import functools
import jax, jax.numpy as jnp
from jax import lax
import numpy as np

D_MODEL = 2048
BATCH = 1
SEQ = 8192
DEPTH = 2
DEC_BATCH = 128
DEC_SEQ = 1
PAST_LEN = 16384
PAGE_SIZE = 128

CONV_DIM = 1024
CONV_WIDTH = 31
CONV_HIST = CONV_WIDTH - 1
POOL_DIM = 1024
POOL_WINDOWS = (2, 4, 8, 16)
POOL_GROUPS = len(POOL_WINDOWS)
POOL_GROUP_DIM = POOL_DIM // POOL_GROUPS
POOL_HIST = max(POOL_WINDOWS) - 1
N_HEADS = 16
QK_NOPE_DIM = 128
QK_ROPE_DIM = 64
V_HEAD_DIM = 128
Q_LORA_RANK = 512
KV_LORA_RANK = 512
ATTN_DIM = N_HEADS * V_HEAD_DIM
ROPE_THETA = 10000.0
SOFTMAX_SCALE = (QK_NOPE_DIM + QK_ROPE_DIM) ** -0.5
Q_BLOCK = 128
NEG_INIT = -1e30
N_BRANCHES = 3
EPS = 1e-6
IN_SPLITS = (2 * CONV_DIM, CONV_DIM, POOL_DIM, POOL_DIM, Q_LORA_RANK, KV_LORA_RANK,
             QK_ROPE_DIM, ATTN_DIM, N_BRANCHES * D_MODEL)
IN_DIM = sum(IN_SPLITS)

kernel_name = "hybrid_conv_pool_mla_gated_decoder_step"

F32 = jnp.float32


def rmsnorm(x, g):
    xf = x.astype(F32)
    y = xf * lax.rsqrt(jnp.mean(xf * xf, axis=-1, keepdims=True) + EPS)
    return (y * g.astype(F32)).astype(x.dtype)


def layernorm(x, g, b):
    xf = x.astype(F32)
    mu = jnp.mean(xf, axis=-1, keepdims=True)
    var = jnp.mean(jnp.square(xf - mu), axis=-1, keepdims=True)
    y = (xf - mu) * lax.rsqrt(var + EPS)
    return (y * g.astype(F32) + b.astype(F32)).astype(x.dtype)


def rope(x, pos):
    half = QK_ROPE_DIM // 2
    freqs = ROPE_THETA ** (-jnp.arange(half, dtype=F32) / half)
    ang = pos.astype(F32)[:, None] * freqs[None, :]
    ang = ang.reshape(ang.shape[:1] + (1,) * (x.ndim - 3) + (half,))
    cos, sin = jnp.cos(ang), jnp.sin(ang)
    xf = x.astype(F32)
    x1, x2 = xf[..., :half], xf[..., half:]
    return jnp.concatenate([x1 * cos - x2 * sin, x2 * cos + x1 * sin], axis=-1).astype(x.dtype)


def split_cols(proj):
    parts, off = [], 0
    for w in IN_SPLITS:
        parts.append(proj[..., off:off + w])
        off += w
    return parts


def conv_branch(a_glu, z, hist, conv_w, conv_b, ln_g, ln_b, w_proj):
    v = a_glu[..., :CONV_DIM] * jax.nn.sigmoid(a_glu[..., CONV_DIM:])
    ext = jnp.concatenate([hist, v], axis=1)
    c = lax.conv_general_dilated(ext, conv_w[:, None, :], window_strides=(1,), padding='VALID',
                                 dimension_numbers=('NWC', 'WIO', 'NWC'),
                                 feature_group_count=CONV_DIM) + conv_b
    y = jax.nn.silu(layernorm(c, ln_g, ln_b)) * jax.nn.silu(z)
    return y @ w_proj, ext[:, -CONV_HIST:]


def pool_branch(u, z, hist, pos, pool_w, pool_scale, w_proj):
    n, t, _ = u.shape
    ext = jnp.concatenate([hist, u], axis=1)
    csum = jnp.cumsum(ext.astype(F32), axis=1)
    p0 = jnp.concatenate([jnp.zeros((n, 1, POOL_DIM), F32), csum], axis=1)
    means = []
    for g, w in enumerate(POOL_WINDOWS):
        sl = slice(g * POOL_GROUP_DIM, (g + 1) * POOL_GROUP_DIM)
        s = p0[:, POOL_HIST + 1:, sl] - p0[:, POOL_HIST + 1 - w:POOL_HIST + 1 - w + t, sl]
        cnt = jnp.minimum(w, pos + 1).astype(F32)
        means.append(s / cnt[None, :, None])
    d = (jnp.concatenate(means, axis=-1) - u.astype(F32)).astype(u.dtype)
    mixed = jnp.einsum('btgc,gcd->btgd', d.reshape(n, t, POOL_GROUPS, POOL_GROUP_DIM), pool_w)
    y = mixed.reshape(n, t, POOL_DIM) * pool_scale * jax.nn.silu(z)
    return y @ w_proj, ext[:, -POOL_HIST:]


def prompt_attention(q_nope, q_rope, latent, k_rope, w_uk, w_uv):
    n, t, h, _ = q_nope.shape
    k_nope = jnp.einsum('bsc,chd->bshd', latent, w_uk)
    v = jnp.einsum('bsc,chd->bshd', latent, w_uv)
    nb = t // Q_BLOCK
    key_pos = jnp.arange(t)

    def to_blocks(a):
        return jnp.moveaxis(a.reshape((n, nb, Q_BLOCK) + a.shape[2:]), 1, 0)

    def block(args):
        qn, qr, i = args
        s = (jnp.einsum('bqhd,bshd->bhqs', qn, k_nope)
             + jnp.einsum('bqhr,bsr->bhqs', qr, k_rope)).astype(F32) * SOFTMAX_SCALE
        q_pos = i * Q_BLOCK + jnp.arange(Q_BLOCK)
        s = jnp.where(key_pos[None, :] <= q_pos[:, None], s, -jnp.inf)
        p = jax.nn.softmax(s, axis=-1).astype(v.dtype)
        return jnp.einsum('bhqs,bshd->bqhd', p, v)

    o = lax.map(block, (to_blocks(q_nope), to_blocks(q_rope), jnp.arange(nb)))
    return jnp.moveaxis(o, 0, 1).reshape(n, t, h * V_HEAD_DIM)


def sample_attention(q_nope, q_rope, latent, k_rope, w_uk, w_uv, ckv_pool, kr_pool, page_table):
    n, t, h, _ = q_nope.shape
    q_lat = jnp.einsum('bthd,chd->bthc', q_nope, w_uk)

    def scores(keys_lat, keys_rope):
        return (jnp.einsum('bthc,bpc->bthp', q_lat, keys_lat)
                + jnp.einsum('bthr,bpr->bthp', q_rope, keys_rope)).astype(F32) * SOFTMAX_SCALE

    def update(carry, s, vals):
        m, l, acc = carry
        m_new = jnp.maximum(m, jnp.max(s, axis=-1))
        alpha = jnp.exp(m - m_new)
        p = jnp.exp(s - m_new[..., None])
        acc = acc * alpha[..., None] + jnp.einsum('bthp,bpc->bthc', p, vals.astype(F32))
        return (m_new, l * alpha + jnp.sum(p, axis=-1), acc)

    def page_step(carry, pages):
        c_blk = ckv_pool[pages]
        r_blk = kr_pool[pages]
        return update(carry, scores(c_blk, r_blk), c_blk), None

    init = (jnp.full((n, t, h), NEG_INIT, F32), jnp.zeros((n, t, h), F32),
            jnp.zeros((n, t, h, KV_LORA_RANK), F32))
    carry, _ = lax.scan(page_step, init, page_table.T)
    s_self = scores(latent, k_rope)
    causal = jnp.arange(t)[None, :] <= jnp.arange(t)[:, None]
    s_self = jnp.where(causal[:, None, :], s_self, -jnp.inf)
    _, l, acc = update(carry, s_self, latent)
    o_lat = (acc / l[..., None]).astype(q_nope.dtype)
    o = jnp.einsum('bthc,chd->bthd', o_lat, w_uv)
    return o.reshape(n, t, h * V_HEAD_DIM)


def mixer_layer(x, pos, hist_conv, hist_pool, attend,
                norm_g, w_in, conv_w, conv_b, conv_ln_g, conv_ln_b, w_conv_out,
                pool_w, pool_scale, w_pool_out,
                q_norm_g, w_q_up, kv_norm_g, w_uk, w_uv, w_attn_out, w_out):
    n, t, _ = x.shape
    h = rmsnorm(x, norm_g)
    (a_glu, z_conv, u_pool, z_pool, c_q, c_kv, k_rope_raw, z_attn, gate_logits) = split_cols(h @ w_in)
    y_conv, new_conv = conv_branch(a_glu, z_conv, hist_conv, conv_w, conv_b, conv_ln_g, conv_ln_b, w_conv_out)
    y_pool, new_pool = pool_branch(u_pool, z_pool, hist_pool, pos, pool_w, pool_scale, w_pool_out)
    q = (rmsnorm(c_q, q_norm_g) @ w_q_up).reshape(n, t, N_HEADS, QK_NOPE_DIM + QK_ROPE_DIM)
    q_nope = q[..., :QK_NOPE_DIM]
    q_rope = rope(q[..., QK_NOPE_DIM:], pos)
    latent = rmsnorm(c_kv, kv_norm_g)
    k_rope = rope(k_rope_raw, pos)
    o = attend(q_nope, q_rope, latent, k_rope, w_uk, w_uv)
    y_attn = (o * jax.nn.silu(z_attn)) @ w_attn_out
    g = jax.nn.sigmoid(gate_logits.astype(F32)).astype(x.dtype).reshape(n, t, N_BRANCHES, D_MODEL)
    merged = g[:, :, 0] * y_conv + g[:, :, 1] * y_pool + g[:, :, 2] * y_attn
    return x + merged @ w_out, new_conv, new_pool, latent, k_rope


def setup_inputs(seed: int = 0) -> dict:
    key = jax.random.key(seed)
    ks = iter(jax.random.split(key, 40))

    def nrm(shape, scale):
        return jax.random.normal(next(ks), shape, F32) * scale

    def gain(shape):
        return 1.0 + nrm(shape, 0.02)

    n_pages = PAST_LEN // PAGE_SIZE
    n_used = DEC_BATCH * n_pages
    n_pool = (5 * n_used) // 4
    perm = jax.random.permutation(next(ks), n_pool)
    page_table = perm[:n_used].reshape(DEC_BATCH, n_pages).astype(jnp.int32)
    return {
        "x_prompt": nrm((BATCH, SEQ, D_MODEL), 1.0),
        "x_sample": nrm((DEC_BATCH, DEC_SEQ, D_MODEL), 1.0),
        "state_conv": nrm((DEPTH, DEC_BATCH, CONV_HIST, CONV_DIM), 0.5),
        "state_pool": nrm((DEPTH, DEC_BATCH, POOL_HIST, POOL_DIM), 1.0),
        "cache_kv_latent": nrm((DEPTH, n_pool, PAGE_SIZE, KV_LORA_RANK), 1.0),
        "cache_k_rope": nrm((DEPTH, n_pool, PAGE_SIZE, QK_ROPE_DIM), 1.0),
        "page_table": page_table,
        "norm_g": gain((DEPTH, D_MODEL)),
        "w_in": nrm((DEPTH, D_MODEL, IN_DIM), D_MODEL ** -0.5),
        "conv_w": nrm((DEPTH, CONV_WIDTH, CONV_DIM), CONV_WIDTH ** -0.5),
        "conv_b": nrm((DEPTH, CONV_DIM), 0.01),
        "conv_ln_g": gain((DEPTH, CONV_DIM)),
        "conv_ln_b": nrm((DEPTH, CONV_DIM), 0.01),
        "w_conv_out": nrm((DEPTH, CONV_DIM, D_MODEL), CONV_DIM ** -0.5),
        "pool_w": nrm((DEPTH, POOL_GROUPS, POOL_GROUP_DIM, POOL_GROUP_DIM), POOL_GROUP_DIM ** -0.5),
        "pool_scale": gain((DEPTH, POOL_DIM)),
        "w_pool_out": nrm((DEPTH, POOL_DIM, D_MODEL), POOL_DIM ** -0.5),
        "q_norm_g": gain((DEPTH, Q_LORA_RANK)),
        "w_q_up": nrm((DEPTH, Q_LORA_RANK, N_HEADS * (QK_NOPE_DIM + QK_ROPE_DIM)), Q_LORA_RANK ** -0.5),
        "kv_norm_g": gain((DEPTH, KV_LORA_RANK)),
        "w_uk": nrm((DEPTH, KV_LORA_RANK, N_HEADS, QK_NOPE_DIM), KV_LORA_RANK ** -0.5),
        "w_uv": nrm((DEPTH, KV_LORA_RANK, N_HEADS, V_HEAD_DIM), KV_LORA_RANK ** -0.5),
        "w_attn_out": nrm((DEPTH, ATTN_DIM, D_MODEL), ATTN_DIM ** -0.5),
        "w_out": nrm((DEPTH, D_MODEL, D_MODEL), D_MODEL ** -0.5),
        "final_norm_g": gain((D_MODEL,)),
    }


def reference(x_prompt, x_sample, state_conv, state_pool, cache_kv_latent, cache_k_rope, page_table,
              norm_g, w_in, conv_w, conv_b, conv_ln_g, conv_ln_b, w_conv_out,
              pool_w, pool_scale, w_pool_out,
              q_norm_g, w_q_up, kv_norm_g, w_uk, w_uv, w_attn_out, w_out, final_norm_g):
    n_p, t_p, _ = x_prompt.shape
    t_s = x_sample.shape[1]
    pos_p = jnp.arange(t_p, dtype=jnp.int32)
    pos_s = PAST_LEN + jnp.arange(t_s, dtype=jnp.int32)
    zero_conv = jnp.zeros((n_p, CONV_HIST, CONV_DIM), x_prompt.dtype)
    zero_pool = jnp.zeros((n_p, POOL_HIST, POOL_DIM), x_prompt.dtype)
    yp, ys = x_prompt, x_sample
    conv_p, pool_p, lat_p, kr_p = [], [], [], []
    conv_s, pool_s, lat_s, kr_s = [], [], [], []
    for l in range(DEPTH):
        weights = (norm_g[l], w_in[l], conv_w[l], conv_b[l], conv_ln_g[l], conv_ln_b[l], w_conv_out[l],
                   pool_w[l], pool_scale[l], w_pool_out[l],
                   q_norm_g[l], w_q_up[l], kv_norm_g[l], w_uk[l], w_uv[l], w_attn_out[l], w_out[l])
        yp, c1, p1, la1, r1 = mixer_layer(yp, pos_p, zero_conv, zero_pool, prompt_attention, *weights)
        attend_s = functools.partial(sample_attention, ckv_pool=cache_kv_latent[l],
                                     kr_pool=cache_k_rope[l], page_table=page_table)
        ys, c2, p2, la2, r2 = mixer_layer(ys, pos_s, state_conv[l], state_pool[l], attend_s, *weights)
        conv_p.append(c1); pool_p.append(p1); lat_p.append(la1); kr_p.append(r1)
        conv_s.append(c2); pool_s.append(p2); lat_s.append(la2); kr_s.append(r2)
    y_prompt = rmsnorm(yp, final_norm_g)
    y_sample = rmsnorm(ys, final_norm_g)
    return (y_prompt, y_sample,
            jnp.stack(conv_p), jnp.stack(pool_p), jnp.stack(lat_p), jnp.stack(kr_p),
            jnp.stack(conv_s), jnp.stack(pool_s), jnp.stack(lat_s), jnp.stack(kr_s))
```

```python
import functools

import jax
import jax.numpy as jnp
from jax import lax
from jax.experimental import pallas as pl
from jax.experimental.pallas import tpu as pltpu

D_MODEL = 2048
DEPTH = 2
PAST_LEN = 16384
PAGE_SIZE = 128
CONV_DIM = 1024
CONV_WIDTH = 31
CONV_HIST = CONV_WIDTH - 1
POOL_DIM = 1024
POOL_WINDOWS = (2, 4, 8, 16)
POOL_GROUP_DIM = POOL_DIM // len(POOL_WINDOWS)
POOL_HIST = max(POOL_WINDOWS) - 1
N_HEADS = 16
QK_NOPE_DIM = 128
QK_ROPE_DIM = 64
V_HEAD_DIM = 128
Q_LORA_RANK = 512
KV_LORA_RANK = 512
ATTN_DIM = N_HEADS * V_HEAD_DIM
ROPE_THETA = 10000.0
SOFTMAX_SCALE = (QK_NOPE_DIM + QK_ROPE_DIM) ** -0.5
NEG_INIT = -1e30
N_BRANCHES = 3
EPS = 1e-6

QK_PAD = 256
ROPE_OFF = QK_NOPE_DIM

OFF_CQ = 2 * CONV_DIM + CONV_DIM + 2 * POOL_DIM
OFF_CKV = OFF_CQ + Q_LORA_RANK
OFF_KR = OFF_CKV + KV_LORA_RANK
OFF_ZATTN = OFF_KR + QK_ROPE_DIM
PROJ_A_DIM = OFF_CQ
PROJ_B_DIM = Q_LORA_RANK + KV_LORA_RANK + 2 * QK_PAD
PROJ_C_DIM = ATTN_DIM + N_BRANCHES * D_MODEL

CONV_HALO = 32
POOL_HALO = 16

VMEM_LIMIT_MIB = 48

F32 = jnp.float32
BF16 = jnp.bfloat16


def _cparams(*sem):
    return pltpu.CompilerParams(dimension_semantics=sem, vmem_limit_bytes=VMEM_LIMIT_MIB << 20)


def _sigmoid(x):
    return 1.0 / (1.0 + jnp.exp(-x))


def _silu(x):
    return x * _sigmoid(x)


def _dot(a, b):
    return jnp.dot(a, b, preferred_element_type=F32)


def _dot_nt(a, b):
    return lax.dot_general(a, b, (((1,), (1,)), ((), ())), preferred_element_type=F32)


def _rmsnorm_kernel(x_ref, g_ref, o_ref):
    x = x_ref[...]
    y = x * lax.rsqrt(jnp.mean(x * x, axis=-1, keepdims=True) + EPS)
    o_ref[...] = (y * g_ref[...]).astype(o_ref.dtype)


def rmsnorm(x, g, out_dtype, bm=512):
    t, d = x.shape
    bm = min(bm, t)
    return pl.pallas_call(
        _rmsnorm_kernel,
        grid=(t // bm,),
        in_specs=[pl.BlockSpec((bm, d), lambda i: (i, 0)),
                  pl.BlockSpec((1, d), lambda i: (0, 0))],
        out_specs=pl.BlockSpec((bm, d), lambda i: (i, 0)),
        out_shape=jax.ShapeDtypeStruct((t, d), out_dtype),
        compiler_params=_cparams("parallel"),
        name="rmsnorm",
    )(x, g.reshape(1, d))


def _mm_kernel(x_ref, w_ref, o_ref):
    o_ref[...] = _dot(x_ref[...], w_ref[...]).astype(o_ref.dtype)


def matmul(x, w, out_dtype, bm=1024, bn=1024, name="matmul"):
    m, k = x.shape
    n = w.shape[1]
    bm = min(bm, m)
    bn = min(bn, n)
    assert m % bm == 0 and n % bn == 0
    return pl.pallas_call(
        _mm_kernel,
        grid=(n // bn, m // bm),
        in_specs=[pl.BlockSpec((bm, k), lambda j, i: (i, 0)),
                  pl.BlockSpec((k, bn), lambda j, i: (0, j))],
        out_specs=pl.BlockSpec((bm, bn), lambda j, i: (i, j)),
        out_shape=jax.ShapeDtypeStruct((m, n), out_dtype),
        compiler_params=_cparams("parallel", "parallel"),
        name=name,
    )(x, w)


def _conv_epilogue(c, z, g_ref, beta_ref):
    mu = jnp.mean(c, axis=-1, keepdims=True)
    cc = c - mu
    var = jnp.mean(cc * cc, axis=-1, keepdims=True)
    ln = cc * lax.rsqrt(var + EPS) * g_ref[...] + beta_ref[...]
    return _silu(ln) * _silu(z)


def _conv_seq_kernel(a_ref, b_ref, z_ref, hist_ref, w_ref, cb_ref, g_ref, beta_ref,
                     y_ref, tail_ref, ext_ref, *, tt):
    i = pl.program_id(0)

    @pl.when(i == 0)
    def _():
        ext_ref[0:CONV_HALO, :] = hist_ref[...]

    @pl.when(i > 0)
    def _():
        ext_ref[0:CONV_HALO, :] = ext_ref[tt:tt + CONV_HALO, :]

    v = a_ref[...] * _sigmoid(b_ref[...])
    ext_ref[CONV_HALO:CONV_HALO + tt, :] = v
    base = CONV_HALO - CONV_HIST
    c = ext_ref[base:base + tt, :] * w_ref[0:1, :] + cb_ref[...]
    for k in range(1, CONV_WIDTH):
        c = c + ext_ref[base + k:base + k + tt, :] * w_ref[k:k + 1, :]
    y_ref[...] = _conv_epilogue(c, z_ref[...], g_ref, beta_ref).astype(y_ref.dtype)
    tail_ref[...] = ext_ref[tt:tt + CONV_HALO, :]


def conv_branch_seq(proj_a, hist, conv_w, conv_b, ln_g, ln_b, tt=256):
    t = proj_a.shape[0]
    c = CONV_DIM
    tt = min(tt, t)
    hist_p = jnp.concatenate([jnp.zeros((CONV_HALO - CONV_HIST, c), F32), hist], axis=0)
    w_p = jnp.concatenate([conv_w, jnp.zeros((CONV_HALO - CONV_WIDTH, c), F32)], axis=0)
    row = lambda a: a.reshape(1, c)
    const = lambda shape: pl.BlockSpec(shape, lambda i: (0, 0))
    y, tail = pl.pallas_call(
        functools.partial(_conv_seq_kernel, tt=tt),
        grid=(t // tt,),
        in_specs=[pl.BlockSpec((tt, c), lambda i: (i, 0)),
                  pl.BlockSpec((tt, c), lambda i: (i, 1)),
                  pl.BlockSpec((tt, c), lambda i: (i, 2)),
                  const((CONV_HALO, c)), const((CONV_HALO, c)),
                  const((1, c)), const((1, c)), const((1, c))],
        out_specs=[pl.BlockSpec((tt, c), lambda i: (i, 0)), const((CONV_HALO, c))],
        out_shape=[jax.ShapeDtypeStruct((t, c), BF16), jax.ShapeDtypeStruct((CONV_HALO, c), F32)],
        scratch_shapes=[pltpu.VMEM((tt + CONV_HALO, c), F32)],
        compiler_params=_cparams("arbitrary"),
        name="conv_seq",
    )(proj_a, proj_a, proj_a, hist_p, w_p, row(conv_b), row(ln_g), row(ln_b))
    return y, tail[CONV_HALO - CONV_HIST:]


def _conv_step_kernel(a_ref, b_ref, z_ref, st_ref, w_ref, cb_ref, g_ref, beta_ref, y_ref, new_ref):
    cdim = CONV_DIM
    v = a_ref[...] * _sigmoid(b_ref[...])
    c = v * w_ref[CONV_HIST:CONV_HIST + 1, :] + cb_ref[...]
    for k in range(CONV_HIST):
        c = c + st_ref[:, k * cdim:(k + 1) * cdim] * w_ref[k:k + 1, :]
    y_ref[...] = _conv_epilogue(c, z_ref[...], g_ref, beta_ref).astype(y_ref.dtype)
    new_ref[:, 0:(CONV_HIST - 1) * cdim] = st_ref[:, cdim:CONV_HIST * cdim]
    new_ref[:, (CONV_HIST - 1) * cdim:] = v


def conv_branch_step(proj_a, state, conv_w, conv_b, ln_g, ln_b, bb=16):
    b = proj_a.shape[0]
    c = CONV_DIM
    bb = min(bb, b)
    st2 = state.reshape(b, CONV_HIST * c)
    w_p = jnp.concatenate([conv_w, jnp.zeros((CONV_HALO - CONV_WIDTH, c), F32)], axis=0)
    row = lambda a: a.reshape(1, c)
    const = lambda shape: pl.BlockSpec(shape, lambda i: (0, 0))
    y, new = pl.pallas_call(
        _conv_step_kernel,
        grid=(b // bb,),
        in_specs=[pl.BlockSpec((bb, c), lambda i: (i, 0)),
                  pl.BlockSpec((bb, c), lambda i: (i, 1)),
                  pl.BlockSpec((bb, c), lambda i: (i, 2)),
                  pl.BlockSpec((bb, CONV_HIST * c), lambda i: (i, 0)),
                  const((CONV_HALO, c)), const((1, c)), const((1, c)), const((1, c))],
        out_specs=[pl.BlockSpec((bb, c), lambda i: (i, 0)),
                   pl.BlockSpec((bb, CONV_HIST * c), lambda i: (i, 0))],
        out_shape=[jax.ShapeDtypeStruct((b, c), BF16),
                   jax.ShapeDtypeStruct((b, CONV_HIST * c), F32)],
        compiler_params=_cparams("parallel"),
        name="conv_step",
    )(proj_a, proj_a, proj_a, st2, w_p, row(conv_b), row(ln_g), row(ln_b))
    return y, new.reshape(b, CONV_HIST, c)


def _pool_mix(s_of_group, u, z, pos, pw_ref, ps_ref, y_ref):
    gd = POOL_GROUP_DIM
    for g, w in enumerate(POOL_WINDOWS):
        sl = slice(g * gd, (g + 1) * gd)
        cnt = jnp.minimum(w, pos + 1).astype(F32)
        d = (s_of_group(g, w) / cnt - u[:, sl]).astype(BF16)
        mixed = _dot(d, pw_ref[g])
        y_ref[:, sl] = (mixed * ps_ref[:, sl] * _silu(z[:, sl])).astype(y_ref.dtype)


def _pool_seq_kernel(u_ref, z_ref, hist_ref, pw_ref, ps_ref, y_ref, tail_ref, ext_ref, *, tt, pos0):
    i = pl.program_id(0)
    gd = POOL_GROUP_DIM

    @pl.when(i == 0)
    def _():
        ext_ref[0:POOL_HALO, :] = hist_ref[...]

    @pl.when(i > 0)
    def _():
        ext_ref[0:POOL_HALO, :] = ext_ref[tt:tt + POOL_HALO, :]

    u = u_ref[...]
    ext_ref[POOL_HALO:POOL_HALO + tt, :] = u
    pos = pos0 + i * tt + lax.broadcasted_iota(jnp.int32, (tt, gd), 0)

    def window_sum(g, w):
        sl = slice(g * gd, (g + 1) * gd)
        s = ext_ref[POOL_HALO:POOL_HALO + tt, sl]
        for j in range(1, w):
            s = s + ext_ref[POOL_HALO - j:POOL_HALO - j + tt, sl]
        return s

    _pool_mix(window_sum, u, z_ref[...], pos, pw_ref, ps_ref, y_ref)
    tail_ref[...] = ext_ref[tt:tt + POOL_HALO, :]


def pool_branch_seq(proj_a, hist, pool_w_bf, pool_scale, pos0, tt=256):
    t = proj_a.shape[0]
    c = POOL_DIM
    tt = min(tt, t)
    hist_p = jnp.concatenate([jnp.zeros((POOL_HALO - POOL_HIST, c), F32), hist], axis=0)
    const2 = lambda shape: pl.BlockSpec(shape, lambda i: (0, 0))
    y, tail = pl.pallas_call(
        functools.partial(_pool_seq_kernel, tt=tt, pos0=pos0),
        grid=(t // tt,),
        in_specs=[pl.BlockSpec((tt, c), lambda i: (i, 3)),
                  pl.BlockSpec((tt, c), lambda i: (i, 4)),
                  const2((POOL_HALO, c)),
                  pl.BlockSpec(pool_w_bf.shape, lambda i: (0, 0, 0)),
                  const2((1, c))],
        out_specs=[pl.BlockSpec((tt, c), lambda i: (i, 0)), const2((POOL_HALO, c))],
        out_shape=[jax.ShapeDtypeStruct((t, c), BF16), jax.ShapeDtypeStruct((POOL_HALO, c), F32)],
        scratch_shapes=[pltpu.VMEM((tt + POOL_HALO, c), F32)],
        compiler_params=_cparams("arbitrary"),
        name="pool_seq",
    )(proj_a, proj_a, hist_p, pool_w_bf, pool_scale.reshape(1, c))
    return y, tail[POOL_HALO - POOL_HIST:]


def _pool_step_kernel(u_ref, z_ref, st_ref, pw_ref, ps_ref, y_ref, new_ref, *, pos0):
    cdim = POOL_DIM
    gd = POOL_GROUP_DIM
    u = u_ref[...]
    pos = jnp.full((u.shape[0], gd), pos0, jnp.int32)

    def window_sum(g, w):
        s = u[:, g * gd:(g + 1) * gd]
        for j in range(1, w):
            k = POOL_HIST - j
            s = s + st_ref[:, k * cdim + g * gd:k * cdim + (g + 1) * gd]
        return s

    _pool_mix(window_sum, u, z_ref[...], pos, pw_ref, ps_ref, y_ref)
    new_ref[:, 0:(POOL_HIST - 1) * cdim] = st_ref[:, cdim:POOL_HIST * cdim]
    new_ref[:, (POOL_HIST - 1) * cdim:] = u


def pool_branch_step(proj_a, state, pool_w_bf, pool_scale, pos0, bb=16):
    b = proj_a.shape[0]
    c = POOL_DIM
    bb = min(bb, b)
    st2 = state.reshape(b, POOL_HIST * c)
    y, new = pl.pallas_call(
        functools.partial(_pool_step_kernel, pos0=pos0),
        grid=(b // bb,),
        in_specs=[pl.BlockSpec((bb, c), lambda i: (i, 3)),
                  pl.BlockSpec((bb, c), lambda i: (i, 4)),
                  pl.BlockSpec((bb, POOL_HIST * c), lambda i: (i, 0)),
                  pl.BlockSpec(pool_w_bf.shape, lambda i: (0, 0, 0)),
                  pl.BlockSpec((1, c), lambda i: (0, 0))],
        out_specs=[pl.BlockSpec((bb, c), lambda i: (i, 0)),
                   pl.BlockSpec((bb, POOL_HIST * c), lambda i: (i, 0))],
        out_shape=[jax.ShapeDtypeStruct((b, c), BF16),
                   jax.ShapeDtypeStruct((b, POOL_HIST * c), F32)],
        compiler_params=_cparams("parallel"),
        name="pool_step",
    )(proj_a, proj_a, st2, pool_w_bf, pool_scale.reshape(1, c))
    return y, new.reshape(b, POOL_HIST, c)


def _rms(x, g):
    return x * lax.rsqrt(jnp.mean(x * x, axis=-1, keepdims=True) + EPS) * g


def _mla_prep_kernel(pb_ref, qg_ref, kvg_ref, ck_ref, sk_ref,
                     qn_ref, lat_ref, latb_ref, kr_ref, kadd_ref):
    q0, q1 = 0, Q_LORA_RANK
    k1 = q1 + KV_LORA_RANK
    r1 = k1 + QK_PAD
    r2 = r1 + QK_PAD
    qn_ref[...] = _rms(pb_ref[:, q0:q1], qg_ref[...]).astype(qn_ref.dtype)
    lat = _rms(pb_ref[:, q1:k1], kvg_ref[...])
    lat_ref[...] = lat
    latb_ref[...] = lat.astype(latb_ref.dtype)
    kadd = pb_ref[:, k1:r1] * ck_ref[...] + pb_ref[:, r1:r2] * sk_ref[...]
    kadd_ref[...] = kadd
    kr_ref[...] = kadd[:, ROPE_OFF:ROPE_OFF + QK_ROPE_DIM]


def mla_prep(proj_b, q_norm_g, kv_norm_g, ck, sk, tt=512):
    t = proj_b.shape[0]
    tt = min(tt, t)
    rowblk = lambda w: pl.BlockSpec((tt, w), lambda i: (i, 0))
    const = lambda w: pl.BlockSpec((1, w), lambda i: (0, 0))
    return pl.pallas_call(
        _mla_prep_kernel,
        grid=(t // tt,),
        in_specs=[rowblk(PROJ_B_DIM), const(Q_LORA_RANK), const(KV_LORA_RANK), rowblk(QK_PAD), rowblk(QK_PAD)],
        out_specs=[rowblk(Q_LORA_RANK), rowblk(KV_LORA_RANK), rowblk(KV_LORA_RANK),
                   rowblk(QK_ROPE_DIM), rowblk(QK_PAD)],
        out_shape=[jax.ShapeDtypeStruct((t, Q_LORA_RANK), BF16),
                   jax.ShapeDtypeStruct((t, KV_LORA_RANK), F32),
                   jax.ShapeDtypeStruct((t, KV_LORA_RANK), BF16),
                   jax.ShapeDtypeStruct((t, QK_ROPE_DIM), F32),
                   jax.ShapeDtypeStruct((t, QK_PAD), F32)],
        compiler_params=_cparams("parallel"),
        name="mla_prep",
    )(proj_b, q_norm_g.reshape(1, -1), kv_norm_g.reshape(1, -1), ck, sk)


def _qproj_kernel(x_ref, w1_ref, w2_ref, c_ref, s_ref, o_ref):
    x = x_ref[...]
    o_ref[0] = (_dot(x, w1_ref[0]) * c_ref[...] + _dot(x, w2_ref[0]) * s_ref[...]).astype(o_ref.dtype)


def q_proj(qn, wq1, wq2, cq, sq, bm=1024):
    t, r = qn.shape
    bm = min(bm, t)
    return pl.pallas_call(
        _qproj_kernel,
        grid=(N_HEADS, t // bm),
        in_specs=[pl.BlockSpec((bm, r), lambda h, i: (i, 0)),
                  pl.BlockSpec((1, r, QK_PAD), lambda h, i: (h, 0, 0)),
                  pl.BlockSpec((1, r, QK_PAD), lambda h, i: (h, 0, 0)),
                  pl.BlockSpec((bm, QK_PAD), lambda h, i: (i, 0)),
                  pl.BlockSpec((bm, QK_PAD), lambda h, i: (i, 0))],
        out_specs=pl.BlockSpec((1, bm, QK_PAD), lambda h, i: (h, i, 0)),
        out_shape=jax.ShapeDtypeStruct((N_HEADS, t, QK_PAD), BF16),
        compiler_params=_cparams("parallel", "parallel"),
        name="q_proj",
    )(qn, wq1, wq2, cq, sq)


def _kcat_kernel(x_ref, w_ref, add_ref, o_ref):
    o_ref[0] = (_dot(x_ref[...], w_ref[0]) + add_ref[...]).astype(o_ref.dtype)


def k_cat(latb, wuk_pad, kadd, bm=1024):
    t, r = latb.shape
    bm = min(bm, t)
    return pl.pallas_call(
        _kcat_kernel,
        grid=(N_HEADS, t // bm),
        in_specs=[pl.BlockSpec((bm, r), lambda h, i: (i, 0)),
                  pl.BlockSpec((1, r, QK_PAD), lambda h, i: (h, 0, 0)),
                  pl.BlockSpec((bm, QK_PAD), lambda h, i: (i, 0))],
        out_specs=pl.BlockSpec((1, bm, QK_PAD), lambda h, i: (h, i, 0)),
        out_shape=jax.ShapeDtypeStruct((N_HEADS, t, QK_PAD), BF16),
        compiler_params=_cparams("parallel", "parallel"),
        name="k_cat",
    )(latb, wuk_pad, kadd)


def _flash_kernel(q_ref, k_ref, v_ref, z_ref, o_ref, *, blk):
    qi = pl.program_id(1)
    q = q_ref[0]

    def step(j, carry, masked):
        m, l, acc = carry
        start = pl.multiple_of(j * blk, blk)
        k = k_ref[0, pl.ds(start, blk), :]
        v = v_ref[pl.ds(start, blk), :]
        s = _dot_nt(q, k) * SOFTMAX_SCALE
        if masked:
            row = lax.broadcasted_iota(jnp.int32, (blk, blk), 0)
            col = lax.broadcasted_iota(jnp.int32, (blk, blk), 1)
            s = jnp.where(col <= row, s, -jnp.inf)
        m_new = jnp.maximum(m, jnp.max(s, axis=-1, keepdims=True))
        alpha = jnp.exp(m - m_new)
        p = jnp.exp(s - m_new)
        l = l * alpha + jnp.sum(p, axis=-1, keepdims=True)
        acc = acc * alpha + _dot(p.astype(BF16), v)
        return m_new, l, acc

    init = (jnp.full((blk, 1), NEG_INIT, F32), jnp.zeros((blk, 1), F32),
            jnp.zeros((blk, V_HEAD_DIM), F32))
    carry = lax.fori_loop(0, qi, lambda j, c: step(j, c, False), init)
    _, l, acc = step(qi, carry, True)
    o_ref[...] = (acc / l * _silu(z_ref[...])).astype(o_ref.dtype)


def flash_attention(qc, kc, v, proj_c, blk=512):
    t = v.shape[0]
    blk = min(blk, t)
    return pl.pallas_call(
        functools.partial(_flash_kernel, blk=blk),
        grid=(N_HEADS, t // blk),
        in_specs=[pl.BlockSpec((1, blk, QK_PAD), lambda h, i: (h, i, 0)),
                  pl.BlockSpec((1, t, QK_PAD), lambda h, i: (h, 0, 0)),
                  pl.BlockSpec((t, V_HEAD_DIM), lambda h, i: (0, h)),
                  pl.BlockSpec((blk, V_HEAD_DIM), lambda h, i: (i, h))],
        out_specs=pl.BlockSpec((blk, V_HEAD_DIM), lambda h, i: (i, h)),
        out_shape=jax.ShapeDtypeStruct((t, ATTN_DIM), BF16),
        compiler_params=_cparams("parallel", "arbitrary"),
        name="flash_attention",
    )(qc, kc, v, proj_c)


def _qlat_kernel(qc_ref, w_ref, o_ref):
    o_ref[0] = _dot(qc_ref[0][:, 0:QK_NOPE_DIM], w_ref[0]).astype(o_ref.dtype)


def q_latent(qc, wuk_t):
    b = qc.shape[1]
    return pl.pallas_call(
        _qlat_kernel,
        grid=(N_HEADS,),
        in_specs=[pl.BlockSpec((1, b, QK_PAD), lambda h: (h, 0, 0)),
                  pl.BlockSpec((1, QK_NOPE_DIM, KV_LORA_RANK), lambda h: (h, 0, 0))],
        out_specs=pl.BlockSpec((1, b, KV_LORA_RANK), lambda h: (h, 0, 0)),
        out_shape=jax.ShapeDtypeStruct((N_HEADS, b, KV_LORA_RANK), BF16),
        compiler_params=_cparams("parallel"),
        name="q_latent",
    )(qc, wuk_t)


def _decode_kernel(pt_ref, ql_ref, qr_ref, sl_ref, sr_ref, *rest, n_pg):
    lat_pages = rest[:n_pg]
    rope_pages = rest[n_pg:2 * n_pg]
    o_ref = rest[2 * n_pg]
    m_sc, l_sc, acc_sc, kbuf, rbuf = rest[2 * n_pg + 1:]
    j = pl.program_id(1)
    ql = ql_ref[0]
    qr = qr_ref[0]

    @pl.when(j == 0)
    def _():
        sl = sl_ref[0].astype(BF16).astype(F32)
        sr = sr_ref[0].astype(BF16).astype(F32)
        s0 = (jnp.sum(ql.astype(F32) * sl, axis=-1, keepdims=True)
              + jnp.sum(qr.astype(F32) * sr, axis=-1, keepdims=True)) * SOFTMAX_SCALE
        m_sc[...] = jnp.maximum(s0, NEG_INIT)
        l_sc[...] = jnp.ones_like(l_sc)
        acc_sc[...] = jnp.broadcast_to(sl, acc_sc.shape)

    for i in range(n_pg):
        kbuf[i * PAGE_SIZE:(i + 1) * PAGE_SIZE, :] = lat_pages[i][...].astype(BF16)
        rbuf[i * PAGE_SIZE:(i + 1) * PAGE_SIZE, :] = rope_pages[i][...].astype(BF16)
    k = kbuf[...]
    s = (_dot_nt(ql, k) + _dot_nt(qr, rbuf[...])) * SOFTMAX_SCALE
    m_prev = m_sc[...]
    m_new = jnp.maximum(m_prev, jnp.max(s, axis=-1, keepdims=True))
    alpha = jnp.exp(m_prev - m_new)
    p = jnp.exp(s - m_new)
    l_sc[...] = l_sc[...] * alpha + jnp.sum(p, axis=-1, keepdims=True)
    acc_sc[...] = acc_sc[...] * alpha + _dot(p.astype(BF16), k)
    m_sc[...] = m_new

    @pl.when(j == pl.num_programs(1) - 1)
    def _():
        o_ref[0] = acc_sc[...] / l_sc[...]


def decode_attention(q_lat, q_rope, lat_new, kr_new, ckv_cache, kr_cache, page_table, layer):
    b, n_pages = page_table.shape
    n_pg = 16 if n_pages % 16 == 0 else (4 if n_pages % 4 == 0 else 1)

    def page_spec(width, i):
        return pl.BlockSpec((None, None, PAGE_SIZE, width),
                            lambda r, j, pt: (layer, pt[r, j * n_pg + i], 0, 0))

    per_row = lambda d1, d2: pl.BlockSpec((1, d1, d2), lambda r, j, pt: (r, 0, 0))
    grid_spec = pltpu.PrefetchScalarGridSpec(
        num_scalar_prefetch=1,
        grid=(b, n_pages // n_pg),
        in_specs=[per_row(N_HEADS, KV_LORA_RANK), per_row(N_HEADS, QK_ROPE_DIM),
                  per_row(1, KV_LORA_RANK), per_row(1, QK_ROPE_DIM)]
                 + [page_spec(KV_LORA_RANK, i) for i in range(n_pg)]
                 + [page_spec(QK_ROPE_DIM, i) for i in range(n_pg)],
        out_specs=per_row(N_HEADS, KV_LORA_RANK),
        scratch_shapes=[pltpu.VMEM((N_HEADS, 1), F32), pltpu.VMEM((N_HEADS, 1), F32),
                        pltpu.VMEM((N_HEADS, KV_LORA_RANK), F32),
                        pltpu.VMEM((n_pg * PAGE_SIZE, KV_LORA_RANK), BF16),
                        pltpu.VMEM((n_pg * PAGE_SIZE, QK_ROPE_DIM), BF16)],
    )
    return pl.pallas_call(
        functools.partial(_decode_kernel, n_pg=n_pg),
        grid_spec=grid_spec,
        out_shape=jax.ShapeDtypeStruct((b, N_HEADS, KV_LORA_RANK), F32),
        compiler_params=_cparams("parallel", "arbitrary"),
        name="decode_attention",
    )(page_table, q_lat, q_rope, lat_new.reshape(b, 1, -1), kr_new.reshape(b, 1, -1),
      *([ckv_cache] * n_pg), *([kr_cache] * n_pg))


def _oproj_kernel(ol_ref, w_ref, z_ref, o_ref):
    o = _dot(ol_ref[0].astype(BF16), w_ref[0])
    o_ref[...] = (o * _silu(z_ref[...])).astype(o_ref.dtype)


def o_proj(o_lat_h, wuv_h, proj_c):
    b = o_lat_h.shape[1]
    return pl.pallas_call(
        _oproj_kernel,
        grid=(N_HEADS,),
        in_specs=[pl.BlockSpec((1, b, KV_LORA_RANK), lambda h: (h, 0, 0)),
                  pl.BlockSpec((1, KV_LORA_RANK, V_HEAD_DIM), lambda h: (h, 0, 0)),
                  pl.BlockSpec((b, V_HEAD_DIM), lambda h: (0, h))],
        out_specs=pl.BlockSpec((b, V_HEAD_DIM), lambda h: (0, h)),
        out_shape=jax.ShapeDtypeStruct((b, ATTN_DIM), BF16),
        compiler_params=_cparams("parallel"),
        name="o_proj",
    )(o_lat_h, wuv_h, proj_c)


def _merge_kernel(yc_ref, yp_ref, ya_ref, wc_ref, wp_ref, wa_ref, g0_ref, g1_ref, g2_ref, o_ref):
    merged = (_sigmoid(g0_ref[...]) * _dot(yc_ref[...], wc_ref[...])
              + _sigmoid(g1_ref[...]) * _dot(yp_ref[...], wp_ref[...])
              + _sigmoid(g2_ref[...]) * _dot(ya_ref[...], wa_ref[...]))
    o_ref[...] = merged.astype(o_ref.dtype)


def gated_merge(yc, yp, ya, wc, wp, wa, proj_c, bm=1024, bn=512):
    t = yc.shape[0]
    d = D_MODEL
    bm = min(bm, t)
    gate_blk0 = ATTN_DIM // bn
    per_branch = d // bn
    act = lambda w: pl.BlockSpec((bm, w), lambda j, i: (i, 0))
    wsp = lambda k: pl.BlockSpec((k, bn), lambda j, i: (0, j))
    gate = lambda br: pl.BlockSpec((bm, bn), lambda j, i: (i, gate_blk0 + br * per_branch + j))
    return pl.pallas_call(
        _merge_kernel,
        grid=(d // bn, t // bm),
        in_specs=[act(CONV_DIM), act(POOL_DIM), act(ATTN_DIM),
                  wsp(CONV_DIM), wsp(POOL_DIM), wsp(ATTN_DIM),
                  gate(0), gate(1), gate(2)],
        out_specs=pl.BlockSpec((bm, bn), lambda j, i: (i, j)),
        out_shape=jax.ShapeDtypeStruct((t, d), BF16),
        compiler_params=_cparams("parallel", "parallel"),
        name="gated_merge",
    )(yc, yp, ya, wc, wp, wa, proj_c, proj_c, proj_c)


def _out_kernel(m_ref, w_ref, x_ref, o_ref):
    o_ref[...] = x_ref[...] + _dot(m_ref[...], w_ref[...])


def out_proj_residual(merged, w_out, x, bm=1024, bn=1024):
    t, d = x.shape
    bm = min(bm, t)
    return pl.pallas_call(
        _out_kernel,
        grid=(d // bn, t // bm),
        in_specs=[pl.BlockSpec((bm, d), lambda j, i: (i, 0)),
                  pl.BlockSpec((d, bn), lambda j, i: (0, j)),
                  pl.BlockSpec((bm, bn), lambda j, i: (i, j))],
        out_specs=pl.BlockSpec((bm, bn), lambda j, i: (i, j)),
        out_shape=jax.ShapeDtypeStruct((t, d), F32),
        compiler_params=_cparams("parallel", "parallel"),
        name="out_proj",
    )(merged, w_out, x)


def _swap_halves(w):
    half = QK_ROPE_DIM // 2
    return jnp.concatenate([w[..., half:], w[..., :half]], axis=-1)


def _prep_layer_weights(w_in, w_q_up, w_uk, w_uv, pool_w, w_conv_out, w_pool_out, w_attn_out, w_out):
    d = w_in.shape[0]
    kr = w_in[:, OFF_KR:OFF_KR + QK_ROPE_DIM]
    zl = jnp.zeros((d, ROPE_OFF), F32)
    zr = jnp.zeros((d, QK_PAD - ROPE_OFF - QK_ROPE_DIM), F32)
    w_b = jnp.concatenate([w_in[:, OFF_CQ:OFF_KR], zl, kr, zr, zl, _swap_halves(kr), zr], axis=1)
    r = w_q_up.shape[0]
    wq = w_q_up.reshape(r, N_HEADS, QK_NOPE_DIM + QK_ROPE_DIM).transpose(1, 0, 2)
    wq_nope, wq_rope = wq[..., :QK_NOPE_DIM], wq[..., QK_NOPE_DIM:]
    zq = jnp.zeros((N_HEADS, r, QK_PAD - ROPE_OFF - QK_ROPE_DIM), F32)
    wq1 = jnp.concatenate([wq_nope, wq_rope, zq], axis=-1)
    wq2 = jnp.concatenate([jnp.zeros_like(wq_nope), _swap_halves(wq_rope), zq], axis=-1)
    wuk_h = w_uk.transpose(1, 0, 2)
    wuk_pad = jnp.concatenate([wuk_h, jnp.zeros((N_HEADS, KV_LORA_RANK, QK_PAD - QK_NOPE_DIM), F32)], axis=-1)
    bf = lambda a: a.astype(BF16)
    return dict(
        w_a=bf(w_in[:, :PROJ_A_DIM]), w_b=bf(w_b), w_c=bf(w_in[:, OFF_ZATTN:]),
        wq1=bf(wq1), wq2=bf(wq2), wuk_pad=bf(wuk_pad),
        wuk_t=bf(w_uk.transpose(1, 2, 0)),
        wuv_flat=bf(w_uv.reshape(KV_LORA_RANK, ATTN_DIM)),
        wuv_h=bf(w_uv.transpose(1, 0, 2)),
        pool_w=bf(pool_w), w_conv_out=bf(w_conv_out), w_pool_out=bf(w_pool_out),
        w_attn_out=bf(w_attn_out), w_out=bf(w_out),
    )


def _rope_tables(pos):
    half = QK_ROPE_DIM // 2
    freqs = ROPE_THETA ** (-jnp.arange(half, dtype=F32) / half)
    ang = pos.astype(F32)[:, None] * freqs[None, :]
    cos, sin = jnp.cos(ang), jnp.sin(ang)
    t = pos.shape[0]
    zl = jnp.zeros((t, ROPE_OFF), F32)
    zr = jnp.zeros((t, QK_PAD - ROPE_OFF - QK_ROPE_DIM), F32)
    ck = jnp.concatenate([zl, cos, cos, zr], axis=1)
    sk = jnp.concatenate([zl, -sin, sin, zr], axis=1)
    cq = jnp.concatenate([jnp.ones((t, ROPE_OFF), F32), cos, cos, zr], axis=1)
    return ck, sk, cq


def _layer_common(x, wts, norm_g, q_norm_g, kv_norm_g, tables):
    ck, sk, cq = tables
    h = rmsnorm(x, norm_g, BF16)
    proj_a = matmul(h, wts["w_a"], F32, name="proj_a")
    proj_b = matmul(h, wts["w_b"], F32, bn=PROJ_B_DIM, name="proj_b")
    proj_c = matmul(h, wts["w_c"], F32, name="proj_c")
    qn, lat, latb, kr, kadd = mla_prep(proj_b, q_norm_g, kv_norm_g, ck, sk)
    qc = q_proj(qn, wts["wq1"], wts["wq2"], cq, sk)
    return proj_a, proj_c, qc, lat, latb, kr, kadd


def _layer_finish(x, y_conv, y_pool, y_attn, proj_c, wts):
    merged = gated_merge(y_conv, y_pool, y_attn, wts["w_conv_out"], wts["w_pool_out"], wts["w_attn_out"], proj_c)
    return out_proj_residual(merged, wts["w_out"], x)


def _prompt_layer(x, wts, p, tables):
    proj_a, proj_c, qc, lat, latb, kr, kadd = _layer_common(x, wts, p["norm_g"], p["q_norm_g"], p["kv_norm_g"], tables)
    y_conv, new_conv = conv_branch_seq(proj_a, jnp.zeros((CONV_HIST, CONV_DIM), F32),
                                       p["conv_w"], p["conv_b"], p["conv_ln_g"], p["conv_ln_b"])
    y_pool, new_pool = pool_branch_seq(proj_a, jnp.zeros((POOL_HIST, POOL_DIM), F32),
                                       wts["pool_w"], p["pool_scale"], 0)
    kc = k_cat(latb, wts["wuk_pad"], kadd)
    v = matmul(latb, wts["wuv_flat"], BF16, name="v_proj")
    y_attn = flash_attention(qc, kc, v, proj_c)
    x_new = _layer_finish(x, y_conv, y_pool, y_attn, proj_c, wts)
    return x_new, new_conv, new_pool, lat, kr


def _sample_layer(x, wts, p, tables, state_conv, state_pool, ckv_cache, kr_cache, page_table, layer):
    proj_a, proj_c, qc, lat, latb, kr, kadd = _layer_common(x, wts, p["norm_g"], p["q_norm_g"], p["kv_norm_g"], tables)
    y_conv, new_conv = conv_branch_step(proj_a, state_conv, p["conv_w"], p["conv_b"], p["conv_ln_g"], p["conv_ln_b"])
    y_pool, new_pool = pool_branch_step(proj_a, state_pool, wts["pool_w"], p["pool_scale"], PAST_LEN)
    q_lat = q_latent(qc, wts["wuk_t"]).transpose(1, 0, 2)
    q_rope = qc[:, :, ROPE_OFF:ROPE_OFF + QK_ROPE_DIM].transpose(1, 0, 2)
    o_lat = decode_attention(q_lat, q_rope, lat, kr, ckv_cache, kr_cache, page_table, layer)
    y_attn = o_proj(o_lat.transpose(1, 0, 2), wts["wuv_h"], proj_c)
    x_new = _layer_finish(x, y_conv, y_pool, y_attn, proj_c, wts)
    return x_new, new_conv, new_pool, lat, kr


def kernel(x_prompt, x_sample, state_conv, state_pool, cache_kv_latent, cache_k_rope, page_table, norm_g, w_in, conv_w, conv_b, conv_ln_g, conv_ln_b, w_conv_out, pool_w, pool_scale, w_pool_out, q_norm_g, w_q_up, kv_norm_g, w_uk, w_uv, w_attn_out, w_out, final_norm_g):
    n_p, t_p, d = x_prompt.shape
    n_s, t_s, _ = x_sample.shape
    assert n_p == 1 and t_s == 1
    depth = w_in.shape[0]
    tables_p = _rope_tables(jnp.arange(t_p, dtype=jnp.int32))
    tables_s = _rope_tables(jnp.full((n_s,), PAST_LEN, jnp.int32))
    yp = x_prompt.reshape(t_p, d)
    ys = x_sample.reshape(n_s, d)
    outs_p, outs_s = [], []
    for l in range(depth):
        wts = _prep_layer_weights(w_in[l], w_q_up[l], w_uk[l], w_uv[l], pool_w[l],
                                  w_conv_out[l], w_pool_out[l], w_attn_out[l], w_out[l])
        p = dict(norm_g=norm_g[l], q_norm_g=q_norm_g[l], kv_norm_g=kv_norm_g[l],
                 conv_w=conv_w[l], conv_b=conv_b[l], conv_ln_g=conv_ln_g[l], conv_ln_b=conv_ln_b[l],
                 pool_scale=pool_scale[l])
        yp, *rest_p = _prompt_layer(yp, wts, p, tables_p)
        ys, *rest_s = _sample_layer(ys, wts, p, tables_s, state_conv[l], state_pool[l],
                                    cache_kv_latent, cache_k_rope, page_table, l)
        outs_p.append(rest_p)
        outs_s.append(rest_s)
    y_prompt = rmsnorm(yp, final_norm_g, F32).reshape(n_p, t_p, d)
    y_sample = rmsnorm(ys, final_norm_g, F32).reshape(n_s, t_s, d)
    stack_p = lambda k: jnp.stack([o[k] for o in outs_p])[:, None]
    stack_s = lambda k: jnp.stack([o[k] for o in outs_s])
    return (y_prompt, y_sample,
            stack_p(0), stack_p(1), stack_p(2), stack_p(3),
            stack_s(0), stack_s(1), stack_s(2)[:, :, None, :], stack_s(3)[:, :, None, :])
```

```python
import functools

import jax
import jax.numpy as jnp
from jax import lax
from jax.experimental import pallas as pl
from jax.experimental.pallas import tpu as pltpu

D_MODEL = 2048
DEPTH = 2
PAST_LEN = 16384
PAGE_SIZE = 128
CONV_DIM = 1024
CONV_WIDTH = 31
CONV_HIST = CONV_WIDTH - 1
POOL_DIM = 1024
POOL_WINDOWS = (2, 4, 8, 16)
POOL_GROUP_DIM = POOL_DIM // len(POOL_WINDOWS)
POOL_HIST = max(POOL_WINDOWS) - 1
N_HEADS = 16
QK_NOPE_DIM = 128
QK_ROPE_DIM = 64
V_HEAD_DIM = 128
Q_LORA_RANK = 512
KV_LORA_RANK = 512
ATTN_DIM = N_HEADS * V_HEAD_DIM
ROPE_THETA = 10000.0
SOFTMAX_SCALE = (QK_NOPE_DIM + QK_ROPE_DIM) ** -0.5
NEG_INIT = -1e30
N_BRANCHES = 3
EPS = 1e-6

QK_PAD = 256
ROPE_OFF = QK_NOPE_DIM

OFF_CQ = 2 * CONV_DIM + CONV_DIM + 2 * POOL_DIM
OFF_CKV = OFF_CQ + Q_LORA_RANK
OFF_KR = OFF_CKV + KV_LORA_RANK
OFF_ZATTN = OFF_KR + QK_ROPE_DIM
PROJ_A_DIM = OFF_CQ
PROJ_B_DIM = Q_LORA_RANK + KV_LORA_RANK + 2 * QK_PAD
PROJ_C_DIM = ATTN_DIM + N_BRANCHES * D_MODEL

CONV_HALO = 32
POOL_HALO = 16

VMEM_LIMIT_MIB = 48

DECODE_PAGES_PER_STEP = 32
DECODE_CHUNKS_PER_STEP = 2

F32 = jnp.float32
BF16 = jnp.bfloat16


def _cparams(*sem):
    return pltpu.CompilerParams(dimension_semantics=sem, vmem_limit_bytes=VMEM_LIMIT_MIB << 20)


def _sigmoid(x):
    return 1.0 / (1.0 + jnp.exp(-x))


def _silu(x):
    return x * _sigmoid(x)


def _dot(a, b):
    return jnp.dot(a, b, preferred_element_type=F32)


def _dot_nt(a, b):
    return lax.dot_general(a, b, (((1,), (1,)), ((), ())), preferred_element_type=F32)


def _rmsnorm_kernel(x_ref, g_ref, o_ref):
    x = x_ref[...]
    y = x * lax.rsqrt(jnp.mean(x * x, axis=-1, keepdims=True) + EPS)
    o_ref[...] = (y * g_ref[...]).astype(o_ref.dtype)


def rmsnorm(x, g, out_dtype, bm=512):
    t, d = x.shape
    bm = min(bm, t)
    return pl.pallas_call(
        _rmsnorm_kernel,
        grid=(t // bm,),
        in_specs=[pl.BlockSpec((bm, d), lambda i: (i, 0)),
                  pl.BlockSpec((1, d), lambda i: (0, 0))],
        out_specs=pl.BlockSpec((bm, d), lambda i: (i, 0)),
        out_shape=jax.ShapeDtypeStruct((t, d), out_dtype),
        compiler_params=_cparams("parallel"),
        name="rmsnorm",
    )(x, g.reshape(1, d))


def _mm_kernel(x_ref, w_ref, o_ref):
    o_ref[...] = _dot(x_ref[...], w_ref[...]).astype(o_ref.dtype)


def matmul(x, w, out_dtype, bm=1024, bn=1024, name="matmul"):
    m, k = x.shape
    n = w.shape[1]
    bm = min(bm, m)
    bn = min(bn, n)
    assert m % bm == 0 and n % bn == 0
    return pl.pallas_call(
        _mm_kernel,
        grid=(n // bn, m // bm),
        in_specs=[pl.BlockSpec((bm, k), lambda j, i: (i, 0)),
                  pl.BlockSpec((k, bn), lambda j, i: (0, j))],
        out_specs=pl.BlockSpec((bm, bn), lambda j, i: (i, j)),
        out_shape=jax.ShapeDtypeStruct((m, n), out_dtype),
        compiler_params=_cparams("parallel", "parallel"),
        name=name,
    )(x, w)


def _conv_epilogue(c, z, g_ref, beta_ref):
    mu = jnp.mean(c, axis=-1, keepdims=True)
    cc = c - mu
    var = jnp.mean(cc * cc, axis=-1, keepdims=True)
    ln = cc * lax.rsqrt(var + EPS) * g_ref[...] + beta_ref[...]
    return _silu(ln) * _silu(z)


def _conv_seq_kernel(a_ref, b_ref, z_ref, hist_ref, w_ref, cb_ref, g_ref, beta_ref,
                     y_ref, tail_ref, ext_ref, *, tt):
    i = pl.program_id(0)

    @pl.when(i == 0)
    def _():
        ext_ref[0:CONV_HALO, :] = hist_ref[...]

    @pl.when(i > 0)
    def _():
        ext_ref[0:CONV_HALO, :] = ext_ref[tt:tt + CONV_HALO, :]

    v = a_ref[...] * _sigmoid(b_ref[...])
    ext_ref[CONV_HALO:CONV_HALO + tt, :] = v
    base = CONV_HALO - CONV_HIST
    c = ext_ref[base:base + tt, :] * w_ref[0:1, :] + cb_ref[...]
    for k in range(1, CONV_WIDTH):
        c = c + ext_ref[base + k:base + k + tt, :] * w_ref[k:k + 1, :]
    y_ref[...] = _conv_epilogue(c, z_ref[...], g_ref, beta_ref).astype(y_ref.dtype)
    tail_ref[...] = ext_ref[tt:tt + CONV_HALO, :]


def conv_branch_seq(proj_a, hist, conv_w, conv_b, ln_g, ln_b, tt=256):
    t = proj_a.shape[0]
    c = CONV_DIM
    tt = min(tt, t)
    hist_p = jnp.concatenate([jnp.zeros((CONV_HALO - CONV_HIST, c), F32), hist], axis=0)
    w_p = jnp.concatenate([conv_w, jnp.zeros((CONV_HALO - CONV_WIDTH, c), F32)], axis=0)
    row = lambda a: a.reshape(1, c)
    const = lambda shape: pl.BlockSpec(shape, lambda i: (0, 0))
    y, tail = pl.pallas_call(
        functools.partial(_conv_seq_kernel, tt=tt),
        grid=(t // tt,),
        in_specs=[pl.BlockSpec((tt, c), lambda i: (i, 0)),
                  pl.BlockSpec((tt, c), lambda i: (i, 1)),
                  pl.BlockSpec((tt, c), lambda i: (i, 2)),
                  const((CONV_HALO, c)), const((CONV_HALO, c)),
                  const((1, c)), const((1, c)), const((1, c))],
        out_specs=[pl.BlockSpec((tt, c), lambda i: (i, 0)), const((CONV_HALO, c))],
        out_shape=[jax.ShapeDtypeStruct((t, c), BF16), jax.ShapeDtypeStruct((CONV_HALO, c), F32)],
        scratch_shapes=[pltpu.VMEM((tt + CONV_HALO, c), F32)],
        compiler_params=_cparams("arbitrary"),
        name="conv_seq",
    )(proj_a, proj_a, proj_a, hist_p, w_p, row(conv_b), row(ln_g), row(ln_b))
    return y, tail[CONV_HALO - CONV_HIST:]


def _conv_step_kernel(a_ref, b_ref, z_ref, st_ref, w_ref, cb_ref, g_ref, beta_ref, y_ref, new_ref):
    cdim = CONV_DIM
    v = a_ref[...] * _sigmoid(b_ref[...])
    c = v * w_ref[CONV_HIST:CONV_HIST + 1, :] + cb_ref[...]
    for k in range(CONV_HIST):
        c = c + st_ref[:, k * cdim:(k + 1) * cdim] * w_ref[k:k + 1, :]
    y_ref[...] = _conv_epilogue(c, z_ref[...], g_ref, beta_ref).astype(y_ref.dtype)
    new_ref[:, 0:(CONV_HIST - 1) * cdim] = st_ref[:, cdim:CONV_HIST * cdim]
    new_ref[:, (CONV_HIST - 1) * cdim:] = v


def conv_branch_step(proj_a, state, conv_w, conv_b, ln_g, ln_b, bb=16):
    b = proj_a.shape[0]
    c = CONV_DIM
    bb = min(bb, b)
    st2 = state.reshape(b, CONV_HIST * c)
    w_p = jnp.concatenate([conv_w, jnp.zeros((CONV_HALO - CONV_WIDTH, c), F32)], axis=0)
    row = lambda a: a.reshape(1, c)
    const = lambda shape: pl.BlockSpec(shape, lambda i: (0, 0))
    y, new = pl.pallas_call(
        _conv_step_kernel,
        grid=(b // bb,),
        in_specs=[pl.BlockSpec((bb, c), lambda i: (i, 0)),
                  pl.BlockSpec((bb, c), lambda i: (i, 1)),
                  pl.BlockSpec((bb, c), lambda i: (i, 2)),
                  pl.BlockSpec((bb, CONV_HIST * c), lambda i: (i, 0)),
                  const((CONV_HALO, c)), const((1, c)), const((1, c)), const((1, c))],
        out_specs=[pl.BlockSpec((bb, c), lambda i: (i, 0)),
                   pl.BlockSpec((bb, CONV_HIST * c), lambda i: (i, 0))],
        out_shape=[jax.ShapeDtypeStruct((b, c), BF16),
                   jax.ShapeDtypeStruct((b, CONV_HIST * c), F32)],
        compiler_params=_cparams("parallel"),
        name="conv_step",
    )(proj_a, proj_a, proj_a, st2, w_p, row(conv_b), row(ln_g), row(ln_b))
    return y, new.reshape(b, CONV_HIST, c)


def _pool_mix(s_of_group, u, z, pos, pw_ref, ps_ref, y_ref):
    gd = POOL_GROUP_DIM
    for g, w in enumerate(POOL_WINDOWS):
        sl = slice(g * gd, (g + 1) * gd)
        cnt = jnp.minimum(w, pos + 1).astype(F32)
        d = (s_of_group(g, w) / cnt - u[:, sl]).astype(BF16)
        mixed = _dot(d, pw_ref[g])
        y_ref[:, sl] = (mixed * ps_ref[:, sl] * _silu(z[:, sl])).astype(y_ref.dtype)


def _pool_seq_kernel(u_ref, z_ref, hist_ref, pw_ref, ps_ref, y_ref, tail_ref, ext_ref, *, tt, pos0):
    i = pl.program_id(0)
    gd = POOL_GROUP_DIM

    @pl.when(i == 0)
    def _():
        ext_ref[0:POOL_HALO, :] = hist_ref[...]

    @pl.when(i > 0)
    def _():
        ext_ref[0:POOL_HALO, :] = ext_ref[tt:tt + POOL_HALO, :]

    u = u_ref[...]
    ext_ref[POOL_HALO:POOL_HALO + tt, :] = u
    pos = pos0 + i * tt + lax.broadcasted_iota(jnp.int32, (tt, gd), 0)

    def window_sum(g, w):
        sl = slice(g * gd, (g + 1) * gd)
        s = ext_ref[POOL_HALO:POOL_HALO + tt, sl]
        for j in range(1, w):
            s = s + ext_ref[POOL_HALO - j:POOL_HALO - j + tt, sl]
        return s

    _pool_mix(window_sum, u, z_ref[...], pos, pw_ref, ps_ref, y_ref)
    tail_ref[...] = ext_ref[tt:tt + POOL_HALO, :]


def pool_branch_seq(proj_a, hist, pool_w_bf, pool_scale, pos0, tt=256):
    t = proj_a.shape[0]
    c = POOL_DIM
    tt = min(tt, t)
    hist_p = jnp.concatenate([jnp.zeros((POOL_HALO - POOL_HIST, c), F32), hist], axis=0)
    const2 = lambda shape: pl.BlockSpec(shape, lambda i: (0, 0))
    y, tail = pl.pallas_call(
        functools.partial(_pool_seq_kernel, tt=tt, pos0=pos0),
        grid=(t // tt,),
        in_specs=[pl.BlockSpec((tt, c), lambda i: (i, 3)),
                  pl.BlockSpec((tt, c), lambda i: (i, 4)),
                  const2((POOL_HALO, c)),
                  pl.BlockSpec(pool_w_bf.shape, lambda i: (0, 0, 0)),
                  const2((1, c))],
        out_specs=[pl.BlockSpec((tt, c), lambda i: (i, 0)), const2((POOL_HALO, c))],
        out_shape=[jax.ShapeDtypeStruct((t, c), BF16), jax.ShapeDtypeStruct((POOL_HALO, c), F32)],
        scratch_shapes=[pltpu.VMEM((tt + POOL_HALO, c), F32)],
        compiler_params=_cparams("arbitrary"),
        name="pool_seq",
    )(proj_a, proj_a, hist_p, pool_w_bf, pool_scale.reshape(1, c))
    return y, tail[POOL_HALO - POOL_HIST:]


def _pool_step_kernel(u_ref, z_ref, st_ref, pw_ref, ps_ref, y_ref, new_ref, *, pos0):
    cdim = POOL_DIM
    gd = POOL_GROUP_DIM
    u = u_ref[...]
    pos = jnp.full((u.shape[0], gd), pos0, jnp.int32)

    def window_sum(g, w):
        s = u[:, g * gd:(g + 1) * gd]
        for j in range(1, w):
            k = POOL_HIST - j
            s = s + st_ref[:, k * cdim + g * gd:k * cdim + (g + 1) * gd]
        return s

    _pool_mix(window_sum, u, z_ref[...], pos, pw_ref, ps_ref, y_ref)
    new_ref[:, 0:(POOL_HIST - 1) * cdim] = st_ref[:, cdim:POOL_HIST * cdim]
    new_ref[:, (POOL_HIST - 1) * cdim:] = u


def pool_branch_step(proj_a, state, pool_w_bf, pool_scale, pos0, bb=16):
    b = proj_a.shape[0]
    c = POOL_DIM
    bb = min(bb, b)
    st2 = state.reshape(b, POOL_HIST * c)
    y, new = pl.pallas_call(
        functools.partial(_pool_step_kernel, pos0=pos0),
        grid=(b // bb,),
        in_specs=[pl.BlockSpec((bb, c), lambda i: (i, 3)),
                  pl.BlockSpec((bb, c), lambda i: (i, 4)),
                  pl.BlockSpec((bb, POOL_HIST * c), lambda i: (i, 0)),
                  pl.BlockSpec(pool_w_bf.shape, lambda i: (0, 0, 0)),
                  pl.BlockSpec((1, c), lambda i: (0, 0))],
        out_specs=[pl.BlockSpec((bb, c), lambda i: (i, 0)),
                   pl.BlockSpec((bb, POOL_HIST * c), lambda i: (i, 0))],
        out_shape=[jax.ShapeDtypeStruct((b, c), BF16),
                   jax.ShapeDtypeStruct((b, POOL_HIST * c), F32)],
        compiler_params=_cparams("parallel"),
        name="pool_step",
    )(proj_a, proj_a, st2, pool_w_bf, pool_scale.reshape(1, c))
    return y, new.reshape(b, POOL_HIST, c)


def _rms(x, g):
    return x * lax.rsqrt(jnp.mean(x * x, axis=-1, keepdims=True) + EPS) * g


def _mla_prep_kernel(pb_ref, qg_ref, kvg_ref, ck_ref, sk_ref,
                     qn_ref, lat_ref, latb_ref, kr_ref, kadd_ref):
    q0, q1 = 0, Q_LORA_RANK
    k1 = q1 + KV_LORA_RANK
    r1 = k1 + QK_PAD
    r2 = r1 + QK_PAD
    qn_ref[...] = _rms(pb_ref[:, q0:q1], qg_ref[...]).astype(qn_ref.dtype)
    lat = _rms(pb_ref[:, q1:k1], kvg_ref[...])
    lat_ref[...] = lat
    latb_ref[...] = lat.astype(latb_ref.dtype)
    kadd = pb_ref[:, k1:r1] * ck_ref[...] + pb_ref[:, r1:r2] * sk_ref[...]
    kadd_ref[...] = kadd
    kr_ref[...] = kadd[:, ROPE_OFF:ROPE_OFF + QK_ROPE_DIM]


def mla_prep(proj_b, q_norm_g, kv_norm_g, ck, sk, tt=512):
    t = proj_b.shape[0]
    tt = min(tt, t)
    rowblk = lambda w: pl.BlockSpec((tt, w), lambda i: (i, 0))
    const = lambda w: pl.BlockSpec((1, w), lambda i: (0, 0))
    return pl.pallas_call(
        _mla_prep_kernel,
        grid=(t // tt,),
        in_specs=[rowblk(PROJ_B_DIM), const(Q_LORA_RANK), const(KV_LORA_RANK), rowblk(QK_PAD), rowblk(QK_PAD)],
        out_specs=[rowblk(Q_LORA_RANK), rowblk(KV_LORA_RANK), rowblk(KV_LORA_RANK),
                   rowblk(QK_ROPE_DIM), rowblk(QK_PAD)],
        out_shape=[jax.ShapeDtypeStruct((t, Q_LORA_RANK), BF16),
                   jax.ShapeDtypeStruct((t, KV_LORA_RANK), F32),
                   jax.ShapeDtypeStruct((t, KV_LORA_RANK), BF16),
                   jax.ShapeDtypeStruct((t, QK_ROPE_DIM), F32),
                   jax.ShapeDtypeStruct((t, QK_PAD), F32)],
        compiler_params=_cparams("parallel"),
        name="mla_prep",
    )(proj_b, q_norm_g.reshape(1, -1), kv_norm_g.reshape(1, -1), ck, sk)


def _qproj_kernel(x_ref, w1_ref, w2_ref, c_ref, s_ref, o_ref):
    x = x_ref[...]
    o_ref[0] = (_dot(x, w1_ref[0]) * c_ref[...] + _dot(x, w2_ref[0]) * s_ref[...]).astype(o_ref.dtype)


def q_proj(qn, wq1, wq2, cq, sq, bm=1024):
    t, r = qn.shape
    bm = min(bm, t)
    return pl.pallas_call(
        _qproj_kernel,
        grid=(N_HEADS, t // bm),
        in_specs=[pl.BlockSpec((bm, r), lambda h, i: (i, 0)),
                  pl.BlockSpec((1, r, QK_PAD), lambda h, i: (h, 0, 0)),
                  pl.BlockSpec((1, r, QK_PAD), lambda h, i: (h, 0, 0)),
                  pl.BlockSpec((bm, QK_PAD), lambda h, i: (i, 0)),
                  pl.BlockSpec((bm, QK_PAD), lambda h, i: (i, 0))],
        out_specs=pl.BlockSpec((1, bm, QK_PAD), lambda h, i: (h, i, 0)),
        out_shape=jax.ShapeDtypeStruct((N_HEADS, t, QK_PAD), BF16),
        compiler_params=_cparams("parallel", "parallel"),
        name="q_proj",
    )(qn, wq1, wq2, cq, sq)


def _kcat_kernel(x_ref, w_ref, add_ref, o_ref):
    o_ref[0] = (_dot(x_ref[...], w_ref[0]) + add_ref[...]).astype(o_ref.dtype)


def k_cat(latb, wuk_pad, kadd, bm=1024):
    t, r = latb.shape
    bm = min(bm, t)
    return pl.pallas_call(
        _kcat_kernel,
        grid=(N_HEADS, t // bm),
        in_specs=[pl.BlockSpec((bm, r), lambda h, i: (i, 0)),
                  pl.BlockSpec((1, r, QK_PAD), lambda h, i: (h, 0, 0)),
                  pl.BlockSpec((bm, QK_PAD), lambda h, i: (i, 0))],
        out_specs=pl.BlockSpec((1, bm, QK_PAD), lambda h, i: (h, i, 0)),
        out_shape=jax.ShapeDtypeStruct((N_HEADS, t, QK_PAD), BF16),
        compiler_params=_cparams("parallel", "parallel"),
        name="k_cat",
    )(latb, wuk_pad, kadd)


def _qproj_t_kernel(x_ref, w1_ref, w2_ref, c_ref, s_ref, o_ref):
    x = x_ref[...]
    o_ref[0] = (_dot_nt(w1_ref[0], x) * c_ref[...] + _dot_nt(w2_ref[0], x) * s_ref[...]).astype(o_ref.dtype)


def q_proj_t(qn, wq1_t, wq2_t, cq_t, sq_t, bm=1024):
    t, r = qn.shape
    bm = min(bm, t)
    return pl.pallas_call(
        _qproj_t_kernel,
        grid=(N_HEADS, t // bm),
        in_specs=[pl.BlockSpec((bm, r), lambda h, i: (i, 0)),
                  pl.BlockSpec((1, QK_PAD, r), lambda h, i: (h, 0, 0)),
                  pl.BlockSpec((1, QK_PAD, r), lambda h, i: (h, 0, 0)),
                  pl.BlockSpec((QK_PAD, bm), lambda h, i: (0, i)),
                  pl.BlockSpec((QK_PAD, bm), lambda h, i: (0, i))],
        out_specs=pl.BlockSpec((1, QK_PAD, bm), lambda h, i: (h, 0, i)),
        out_shape=jax.ShapeDtypeStruct((N_HEADS, QK_PAD, t), BF16),
        compiler_params=_cparams("parallel", "parallel"),
        name="q_proj_t",
    )(qn, wq1_t, wq2_t, cq_t, sq_t)


def _vproj_t_kernel(x_ref, w_ref, o_ref):
    o_ref[0] = _dot_nt(w_ref[0], x_ref[...]).astype(o_ref.dtype)


def v_proj_t(latb, wuv_t, bm=1024):
    t, r = latb.shape
    bm = min(bm, t)
    return pl.pallas_call(
        _vproj_t_kernel,
        grid=(N_HEADS, t // bm),
        in_specs=[pl.BlockSpec((bm, r), lambda h, i: (i, 0)),
                  pl.BlockSpec((1, V_HEAD_DIM, r), lambda h, i: (h, 0, 0))],
        out_specs=pl.BlockSpec((1, V_HEAD_DIM, bm), lambda h, i: (h, 0, i)),
        out_shape=jax.ShapeDtypeStruct((N_HEADS, V_HEAD_DIM, t), BF16),
        compiler_params=_cparams("parallel", "parallel"),
        name="v_proj_t",
    )(latb, wuv_t)


EXP2_SCALE = SOFTMAX_SCALE * 1.4426950408889634


def _flash_kernel(qt_ref, k_ref, vt_ref, z_ref, o_ref, sa_ref, sb_ref, *, blk):
    qi = pl.program_id(1)
    qt = qt_ref[0]

    def scores_into(j, s_ref):
        start = pl.multiple_of(j * blk, blk)
        s_ref[...] = _dot(k_ref[0, pl.ds(start, blk), :], qt)

    def update(j, s_ref, carry, masked=False):
        m, l, acc = carry
        s = s_ref[...]
        if masked:
            key = lax.broadcasted_iota(jnp.int32, (blk, blk), 0)
            qry = lax.broadcasted_iota(jnp.int32, (blk, blk), 1)
            s = jnp.where(key <= qry, s, -jnp.inf)
        m_new = jnp.maximum(m, jnp.max(s, axis=0, keepdims=True))
        alpha = jnp.exp2((m - m_new) * EXP2_SCALE)
        p = jnp.exp2((s - m_new) * EXP2_SCALE)
        l = l * alpha + jnp.sum(p, axis=0, keepdims=True)
        start = pl.multiple_of(j * blk, blk)
        vt = vt_ref[0, :, pl.ds(start, blk)]
        acc = acc * alpha + _dot(vt, p.astype(BF16))
        return m_new, l, acc

    def finish(carry):
        _, l, acc = carry
        o_ref[...] = ((acc / l).T * _silu(z_ref[...])).astype(o_ref.dtype)

    scores_into(0, sa_ref)

    def pair(t, carry):
        scores_into(2 * t + 1, sb_ref)
        carry = update(2 * t, sa_ref, carry)
        scores_into(2 * t + 2, sa_ref)
        return update(2 * t + 1, sb_ref, carry)

    init = (jnp.full((1, blk), NEG_INIT, F32), jnp.zeros((1, blk), F32), jnp.zeros((V_HEAD_DIM, blk), F32))
    carry = lax.fori_loop(0, qi // 2, pair, init)

    @pl.when(qi % 2 == 0)
    def _():
        finish(update(qi, sa_ref, carry, masked=True))

    @pl.when(qi % 2 == 1)
    def _():
        scores_into(qi, sb_ref)
        finish(update(qi, sb_ref, update(qi - 1, sa_ref, carry), masked=True))


def flash_attention(qt, kc, vt, proj_c, blk=512):
    t = kc.shape[1]
    blk = min(blk, t)
    return pl.pallas_call(
        functools.partial(_flash_kernel, blk=blk),
        grid=(N_HEADS, t // blk),
        in_specs=[pl.BlockSpec((1, QK_PAD, blk), lambda h, i: (h, 0, i)),
                  pl.BlockSpec((1, t, QK_PAD), lambda h, i: (h, 0, 0)),
                  pl.BlockSpec((1, V_HEAD_DIM, t), lambda h, i: (h, 0, 0)),
                  pl.BlockSpec((blk, V_HEAD_DIM), lambda h, i: (i, h))],
        out_specs=pl.BlockSpec((blk, V_HEAD_DIM), lambda h, i: (i, h)),
        out_shape=jax.ShapeDtypeStruct((t, ATTN_DIM), BF16),
        scratch_shapes=[pltpu.VMEM((blk, blk), F32), pltpu.VMEM((blk, blk), F32)],
        compiler_params=_cparams("parallel", "arbitrary"),
        name="flash_attention",
    )(qt, kc, vt, proj_c)


def _qlat_kernel(qc_ref, w_ref, o_ref):
    o_ref[0] = _dot(qc_ref[0][:, 0:QK_NOPE_DIM], w_ref[0]).astype(o_ref.dtype)


def q_latent(qc, wuk_t):
    b = qc.shape[1]
    return pl.pallas_call(
        _qlat_kernel,
        grid=(N_HEADS,),
        in_specs=[pl.BlockSpec((1, b, QK_PAD), lambda h: (h, 0, 0)),
                  pl.BlockSpec((1, QK_NOPE_DIM, KV_LORA_RANK), lambda h: (h, 0, 0))],
        out_specs=pl.BlockSpec((1, b, KV_LORA_RANK), lambda h: (h, 0, 0)),
        out_shape=jax.ShapeDtypeStruct((N_HEADS, b, KV_LORA_RANK), BF16),
        compiler_params=_cparams("parallel"),
        name="q_latent",
    )(qc, wuk_t)


def _decode_kernel(pt_ref, ql_ref, qr_ref, sl_ref, sr_ref, *rest, n_pg, n_ch):
    lat_pages = rest[:n_pg]
    rope_pages = rest[n_pg:2 * n_pg]
    o_ref = rest[2 * n_pg]
    m_sc, l_sc, acc_sc, kbuf, rbuf = rest[2 * n_pg + 1:]
    cp = n_pg // n_ch
    j = pl.program_id(1)
    ql = ql_ref[0]
    qr = qr_ref[0]

    @pl.when(j == 0)
    def _():
        sl = sl_ref[0].astype(BF16).astype(F32)
        sr = sr_ref[0].astype(BF16).astype(F32)
        s0 = (jnp.sum(ql.astype(F32) * sl, axis=-1, keepdims=True)
              + jnp.sum(qr.astype(F32) * sr, axis=-1, keepdims=True)) * SOFTMAX_SCALE
        m_sc[...] = jnp.maximum(s0, NEG_INIT)
        l_sc[...] = jnp.ones_like(l_sc)
        acc_sc[...] = jnp.broadcast_to(sl, acc_sc.shape)

    parts = []
    for c in range(n_ch):
        for i in range(cp):
            kbuf[c, i * PAGE_SIZE:(i + 1) * PAGE_SIZE, :] = lat_pages[c * cp + i][...].astype(BF16)
            rbuf[c, :, i * PAGE_SIZE:(i + 1) * PAGE_SIZE] = rope_pages[c * cp + i][...].astype(BF16)
        k = kbuf[c]
        s = (_dot_nt(ql, k) + _dot(qr, rbuf[c])) * SOFTMAX_SCALE
        m_c = jnp.max(s, axis=-1, keepdims=True)
        p = jnp.exp(s - m_c)
        parts.append((m_c, jnp.sum(p, axis=-1, keepdims=True), _dot(p.astype(BF16), k)))
    m_prev = m_sc[...]
    m_new = m_prev
    for m_c, _, _ in parts:
        m_new = jnp.maximum(m_new, m_c)
    alpha = jnp.exp(m_prev - m_new)
    l = l_sc[...] * alpha
    acc = acc_sc[...] * alpha
    for m_c, l_c, acc_c in parts:
        w = jnp.exp(m_c - m_new)
        l = l + l_c * w
        acc = acc + acc_c * w
    l_sc[...] = l
    acc_sc[...] = acc
    m_sc[...] = m_new

    @pl.when(j == pl.num_programs(1) - 1)
    def _():
        o_ref[0] = acc_sc[...] / l_sc[...]


def decode_attention(q_lat, q_rope, lat_new, kr_new, ckv_cache, kr_cache_t, page_table, layer):
    b, n_pages = page_table.shape
    n_pg = next(n for n in (DECODE_PAGES_PER_STEP, 8, 2, 1) if n_pages % n == 0)
    n_ch = min(DECODE_CHUNKS_PER_STEP, n_pg)
    cp = n_pg // n_ch

    def page_spec(d1, d2, i):
        return pl.BlockSpec((None, None, d1, d2),
                            lambda r, j, pt: (layer, pt[r, j * n_pg + i], 0, 0))

    per_row = lambda d1, d2: pl.BlockSpec((1, d1, d2), lambda r, j, pt: (r, 0, 0))
    grid_spec = pltpu.PrefetchScalarGridSpec(
        num_scalar_prefetch=1,
        grid=(b, n_pages // n_pg),
        in_specs=[per_row(N_HEADS, KV_LORA_RANK), per_row(N_HEADS, QK_ROPE_DIM),
                  per_row(1, KV_LORA_RANK), per_row(1, QK_ROPE_DIM)]
                 + [page_spec(PAGE_SIZE, KV_LORA_RANK, i) for i in range(n_pg)]
                 + [page_spec(QK_ROPE_DIM, PAGE_SIZE, i) for i in range(n_pg)],
        out_specs=per_row(N_HEADS, KV_LORA_RANK),
        scratch_shapes=[pltpu.VMEM((N_HEADS, 1), F32), pltpu.VMEM((N_HEADS, 1), F32),
                        pltpu.VMEM((N_HEADS, KV_LORA_RANK), F32),
                        pltpu.VMEM((n_ch, cp * PAGE_SIZE, KV_LORA_RANK), BF16),
                        pltpu.VMEM((n_ch, QK_ROPE_DIM, cp * PAGE_SIZE), BF16)],
    )
    return pl.pallas_call(
        functools.partial(_decode_kernel, n_pg=n_pg, n_ch=n_ch),
        grid_spec=grid_spec,
        out_shape=jax.ShapeDtypeStruct((b, N_HEADS, KV_LORA_RANK), F32),
        compiler_params=_cparams("parallel", "arbitrary"),
        name="decode_attention",
    )(page_table, q_lat, q_rope, lat_new.reshape(b, 1, -1), kr_new.reshape(b, 1, -1),
      *([ckv_cache] * n_pg), *([kr_cache_t] * n_pg))


def _oproj_kernel(ol_ref, w_ref, z_ref, o_ref):
    o = _dot(ol_ref[0].astype(BF16), w_ref[0])
    o_ref[...] = (o * _silu(z_ref[...])).astype(o_ref.dtype)


def o_proj(o_lat_h, wuv_h, proj_c):
    b = o_lat_h.shape[1]
    return pl.pallas_call(
        _oproj_kernel,
        grid=(N_HEADS,),
        in_specs=[pl.BlockSpec((1, b, KV_LORA_RANK), lambda h: (h, 0, 0)),
                  pl.BlockSpec((1, KV_LORA_RANK, V_HEAD_DIM), lambda h: (h, 0, 0)),
                  pl.BlockSpec((b, V_HEAD_DIM), lambda h: (0, h))],
        out_specs=pl.BlockSpec((b, V_HEAD_DIM), lambda h: (0, h)),
        out_shape=jax.ShapeDtypeStruct((b, ATTN_DIM), BF16),
        compiler_params=_cparams("parallel"),
        name="o_proj",
    )(o_lat_h, wuv_h, proj_c)


def _merge_kernel(yc_ref, yp_ref, ya_ref, wc_ref, wp_ref, wa_ref, g0_ref, g1_ref, g2_ref, o_ref):
    merged = (_sigmoid(g0_ref[...]) * _dot(yc_ref[...], wc_ref[...])
              + _sigmoid(g1_ref[...]) * _dot(yp_ref[...], wp_ref[...])
              + _sigmoid(g2_ref[...]) * _dot(ya_ref[...], wa_ref[...]))
    o_ref[...] = merged.astype(o_ref.dtype)


def gated_merge(yc, yp, ya, wc, wp, wa, proj_c, bm=1024, bn=512):
    t = yc.shape[0]
    d = D_MODEL
    bm = min(bm, t)
    gate_blk0 = ATTN_DIM // bn
    per_branch = d // bn
    act = lambda w: pl.BlockSpec((bm, w), lambda j, i: (i, 0))
    wsp = lambda k: pl.BlockSpec((k, bn), lambda j, i: (0, j))
    gate = lambda br: pl.BlockSpec((bm, bn), lambda j, i: (i, gate_blk0 + br * per_branch + j))
    return pl.pallas_call(
        _merge_kernel,
        grid=(d // bn, t // bm),
        in_specs=[act(CONV_DIM), act(POOL_DIM), act(ATTN_DIM),
                  wsp(CONV_DIM), wsp(POOL_DIM), wsp(ATTN_DIM),
                  gate(0), gate(1), gate(2)],
        out_specs=pl.BlockSpec((bm, bn), lambda j, i: (i, j)),
        out_shape=jax.ShapeDtypeStruct((t, d), BF16),
        compiler_params=_cparams("parallel", "parallel"),
        name="gated_merge",
    )(yc, yp, ya, wc, wp, wa, proj_c, proj_c, proj_c)


def _out_kernel(m_ref, w_ref, x_ref, o_ref):
    o_ref[...] = x_ref[...] + _dot(m_ref[...], w_ref[...])


def out_proj_residual(merged, w_out, x, bm=1024, bn=1024):
    t, d = x.shape
    bm = min(bm, t)
    return pl.pallas_call(
        _out_kernel,
        grid=(d // bn, t // bm),
        in_specs=[pl.BlockSpec((bm, d), lambda j, i: (i, 0)),
                  pl.BlockSpec((d, bn), lambda j, i: (0, j)),
                  pl.BlockSpec((bm, bn), lambda j, i: (i, j))],
        out_specs=pl.BlockSpec((bm, bn), lambda j, i: (i, j)),
        out_shape=jax.ShapeDtypeStruct((t, d), F32),
        compiler_params=_cparams("parallel", "parallel"),
        name="out_proj",
    )(merged, w_out, x)


def _swap_halves(w):
    half = QK_ROPE_DIM // 2
    return jnp.concatenate([w[..., half:], w[..., :half]], axis=-1)


def _prep_layer_weights(w_in, w_q_up, w_uk, w_uv, pool_w, w_conv_out, w_pool_out, w_attn_out, w_out):
    d = w_in.shape[0]
    kr = w_in[:, OFF_KR:OFF_KR + QK_ROPE_DIM]
    zl = jnp.zeros((d, ROPE_OFF), F32)
    zr = jnp.zeros((d, QK_PAD - ROPE_OFF - QK_ROPE_DIM), F32)
    w_b = jnp.concatenate([w_in[:, OFF_CQ:OFF_KR], zl, kr, zr, zl, _swap_halves(kr), zr], axis=1)
    r = w_q_up.shape[0]
    wq = w_q_up.reshape(r, N_HEADS, QK_NOPE_DIM + QK_ROPE_DIM).transpose(1, 0, 2)
    wq_nope, wq_rope = wq[..., :QK_NOPE_DIM], wq[..., QK_NOPE_DIM:]
    zq = jnp.zeros((N_HEADS, r, QK_PAD - ROPE_OFF - QK_ROPE_DIM), F32)
    wq1 = jnp.concatenate([wq_nope, wq_rope, zq], axis=-1)
    wq2 = jnp.concatenate([jnp.zeros_like(wq_nope), _swap_halves(wq_rope), zq], axis=-1)
    wuk_h = w_uk.transpose(1, 0, 2)
    wuk_pad = jnp.concatenate([wuk_h, jnp.zeros((N_HEADS, KV_LORA_RANK, QK_PAD - QK_NOPE_DIM), F32)], axis=-1)
    bf = lambda a: a.astype(BF16)
    return dict(
        w_a=bf(w_in[:, :PROJ_A_DIM]), w_b=bf(w_b), w_c=bf(w_in[:, OFF_ZATTN:]),
        wq1=bf(wq1), wq2=bf(wq2), wuk_pad=bf(wuk_pad),
        wq1_t=bf(wq1.transpose(0, 2, 1)), wq2_t=bf(wq2.transpose(0, 2, 1)),
        wuk_t=bf(w_uk.transpose(1, 2, 0)),
        wuv_t=bf(w_uv.transpose(1, 2, 0)),
        wuv_h=bf(w_uv.transpose(1, 0, 2)),
        pool_w=bf(pool_w), w_conv_out=bf(w_conv_out), w_pool_out=bf(w_pool_out),
        w_attn_out=bf(w_attn_out), w_out=bf(w_out),
    )


def _rope_tables(pos):
    half = QK_ROPE_DIM // 2
    freqs = ROPE_THETA ** (-jnp.arange(half, dtype=F32) / half)
    ang = pos.astype(F32)[:, None] * freqs[None, :]
    cos, sin = jnp.cos(ang), jnp.sin(ang)
    t = pos.shape[0]
    zl = jnp.zeros((t, ROPE_OFF), F32)
    zr = jnp.zeros((t, QK_PAD - ROPE_OFF - QK_ROPE_DIM), F32)
    ck = jnp.concatenate([zl, cos, cos, zr], axis=1)
    sk = jnp.concatenate([zl, -sin, sin, zr], axis=1)
    cq = jnp.concatenate([jnp.ones((t, ROPE_OFF), F32), cos, cos, zr], axis=1)
    return ck, sk, cq


def _layer_common(x, wts, norm_g, q_norm_g, kv_norm_g, tables):
    ck, sk, cq = tables
    h = rmsnorm(x, norm_g, BF16)
    proj_a = matmul(h, wts["w_a"], F32, name="proj_a")
    proj_b = matmul(h, wts["w_b"], F32, bn=PROJ_B_DIM, name="proj_b")
    proj_c = matmul(h, wts["w_c"], F32, name="proj_c")
    qn, lat, latb, kr, kadd = mla_prep(proj_b, q_norm_g, kv_norm_g, ck, sk)
    return proj_a, proj_c, qn, lat, latb, kr, kadd


def _layer_finish(x, y_conv, y_pool, y_attn, proj_c, wts):
    merged = gated_merge(y_conv, y_pool, y_attn, wts["w_conv_out"], wts["w_pool_out"], wts["w_attn_out"], proj_c)
    return out_proj_residual(merged, wts["w_out"], x)


def _prompt_layer(x, wts, p, tables):
    proj_a, proj_c, qn, lat, latb, kr, kadd = _layer_common(x, wts, p["norm_g"], p["q_norm_g"], p["kv_norm_g"], tables)
    y_conv, new_conv = conv_branch_seq(proj_a, jnp.zeros((CONV_HIST, CONV_DIM), F32),
                                       p["conv_w"], p["conv_b"], p["conv_ln_g"], p["conv_ln_b"])
    y_pool, new_pool = pool_branch_seq(proj_a, jnp.zeros((POOL_HIST, POOL_DIM), F32),
                                       wts["pool_w"], p["pool_scale"], 0)
    _, sk, cq = tables
    qt = q_proj_t(qn, wts["wq1_t"], wts["wq2_t"], cq.T, sk.T)
    kc = k_cat(latb, wts["wuk_pad"], kadd)
    vt = v_proj_t(latb, wts["wuv_t"])
    y_attn = flash_attention(qt, kc, vt, proj_c)
    x_new = _layer_finish(x, y_conv, y_pool, y_attn, proj_c, wts)
    return x_new, new_conv, new_pool, lat, kr


def _sample_layer(x, wts, p, tables, state_conv, state_pool, ckv_cache, kr_cache_t, page_table, layer):
    proj_a, proj_c, qn, lat, latb, kr, kadd = _layer_common(x, wts, p["norm_g"], p["q_norm_g"], p["kv_norm_g"], tables)
    y_conv, new_conv = conv_branch_step(proj_a, state_conv, p["conv_w"], p["conv_b"], p["conv_ln_g"], p["conv_ln_b"])
    y_pool, new_pool = pool_branch_step(proj_a, state_pool, wts["pool_w"], p["pool_scale"], PAST_LEN)
    _, sk, cq = tables
    qc = q_proj(qn, wts["wq1"], wts["wq2"], cq, sk)
    q_lat = q_latent(qc, wts["wuk_t"]).transpose(1, 0, 2)
    q_rope = qc[:, :, ROPE_OFF:ROPE_OFF + QK_ROPE_DIM].transpose(1, 0, 2)
    o_lat = decode_attention(q_lat, q_rope, lat, kr, ckv_cache, kr_cache_t, page_table, layer)
    y_attn = o_proj(o_lat.transpose(1, 0, 2), wts["wuv_h"], proj_c)
    x_new = _layer_finish(x, y_conv, y_pool, y_attn, proj_c, wts)
    return x_new, new_conv, new_pool, lat, kr


def kernel(x_prompt, x_sample, state_conv, state_pool, cache_kv_latent, cache_k_rope, page_table, norm_g, w_in, conv_w, conv_b, conv_ln_g, conv_ln_b, w_conv_out, pool_w, pool_scale, w_pool_out, q_norm_g, w_q_up, kv_norm_g, w_uk, w_uv, w_attn_out, w_out, final_norm_g):
    n_p, t_p, d = x_prompt.shape
    n_s, t_s, _ = x_sample.shape
    assert n_p == 1 and t_s == 1
    depth = w_in.shape[0]
    tables_p = _rope_tables(jnp.arange(t_p, dtype=jnp.int32))
    tables_s = _rope_tables(jnp.full((n_s,), PAST_LEN, jnp.int32))
    yp = x_prompt.reshape(t_p, d)
    ys = x_sample.reshape(n_s, d)
    kr_cache_t = jnp.swapaxes(cache_k_rope, 2, 3)
    outs_p, outs_s = [], []
    for l in range(depth):
        wts = _prep_layer_weights(w_in[l], w_q_up[l], w_uk[l], w_uv[l], pool_w[l],
                                  w_conv_out[l], w_pool_out[l], w_attn_out[l], w_out[l])
        p = dict(norm_g=norm_g[l], q_norm_g=q_norm_g[l], kv_norm_g=kv_norm_g[l],
                 conv_w=conv_w[l], conv_b=conv_b[l], conv_ln_g=conv_ln_g[l], conv_ln_b=conv_ln_b[l],
                 pool_scale=pool_scale[l])
        yp, *rest_p = _prompt_layer(yp, wts, p, tables_p)
        ys, *rest_s = _sample_layer(ys, wts, p, tables_s, state_conv[l], state_pool[l],
                                    cache_kv_latent, kr_cache_t, page_table, l)
        outs_p.append(rest_p)
        outs_s.append(rest_s)
    y_prompt = rmsnorm(yp, final_norm_g, F32).reshape(n_p, t_p, d)
    y_sample = rmsnorm(ys, final_norm_g, F32).reshape(n_s, t_s, d)
    stack_p = lambda k: jnp.stack([o[k] for o in outs_p])[:, None]
    stack_s = lambda k: jnp.stack([o[k] for o in outs_s])
    return (y_prompt, y_sample,
            stack_p(0), stack_p(1), stack_p(2), stack_p(3),
            stack_s(0), stack_s(1), stack_s(2)[:, :, None, :], stack_s(3)[:, :, None, :])
```

```python
import functools

import jax
import jax.numpy as jnp
from jax import lax
from jax.experimental import pallas as pl
from jax.experimental.pallas import tpu as pltpu

D_MODEL = 2048
DEPTH = 2
PAST_LEN = 16384
PAGE_SIZE = 128
CONV_DIM = 1024
CONV_WIDTH = 31
CONV_HIST = CONV_WIDTH - 1
POOL_DIM = 1024
POOL_WINDOWS = (2, 4, 8, 16)
POOL_GROUP_DIM = POOL_DIM // len(POOL_WINDOWS)
POOL_HIST = max(POOL_WINDOWS) - 1
N_HEADS = 16
QK_NOPE_DIM = 128
QK_ROPE_DIM = 64
V_HEAD_DIM = 128
Q_LORA_RANK = 512
KV_LORA_RANK = 512
ATTN_DIM = N_HEADS * V_HEAD_DIM
ROPE_THETA = 10000.0
SOFTMAX_SCALE = (QK_NOPE_DIM + QK_ROPE_DIM) ** -0.5
NEG_INIT = -1e30
N_BRANCHES = 3
EPS = 1e-6

QK_PAD = 256
ROPE_OFF = QK_NOPE_DIM

OFF_CQ = 2 * CONV_DIM + CONV_DIM + 2 * POOL_DIM
OFF_CKV = OFF_CQ + Q_LORA_RANK
OFF_KR = OFF_CKV + KV_LORA_RANK
OFF_ZATTN = OFF_KR + QK_ROPE_DIM
PROJ_A_DIM = OFF_CQ
PROJ_B_DIM = Q_LORA_RANK + KV_LORA_RANK + 2 * QK_PAD
PROJ_C_DIM = ATTN_DIM + N_BRANCHES * D_MODEL

CONV_HALO = 32
POOL_HALO = 16

VMEM_LIMIT_MIB = 48

DECODE_PAGES_PER_STEP = 32
DECODE_CHUNKS_PER_STEP = 2

F32 = jnp.float32
BF16 = jnp.bfloat16


def _cparams(*sem):
    return pltpu.CompilerParams(dimension_semantics=sem, vmem_limit_bytes=VMEM_LIMIT_MIB << 20)


def _sigmoid(x):
    return 1.0 / (1.0 + jnp.exp(-x))


def _silu(x):
    return x * _sigmoid(x)


def _dot(a, b):
    return jnp.dot(a, b, preferred_element_type=F32)


def _dot_nt(a, b):
    return lax.dot_general(a, b, (((1,), (1,)), ((), ())), preferred_element_type=F32)


def _rmsnorm_kernel(x_ref, g_ref, o_ref):
    x = x_ref[...]
    y = x * lax.rsqrt(jnp.mean(x * x, axis=-1, keepdims=True) + EPS)
    o_ref[...] = (y * g_ref[...]).astype(o_ref.dtype)


def rmsnorm(x, g, out_dtype, bm=512):
    t, d = x.shape
    bm = min(bm, t)
    return pl.pallas_call(
        _rmsnorm_kernel,
        grid=(t // bm,),
        in_specs=[pl.BlockSpec((bm, d), lambda i: (i, 0)),
                  pl.BlockSpec((1, d), lambda i: (0, 0))],
        out_specs=pl.BlockSpec((bm, d), lambda i: (i, 0)),
        out_shape=jax.ShapeDtypeStruct((t, d), out_dtype),
        compiler_params=_cparams("parallel"),
        name="rmsnorm",
    )(x, g.reshape(1, d))


def _mm_kernel(x_ref, w_ref, o_ref):
    o_ref[...] = _dot(x_ref[...], w_ref[...]).astype(o_ref.dtype)


def matmul(x, w, out_dtype, bm=1024, bn=1024, name="matmul"):
    m, k = x.shape
    n = w.shape[1]
    bm = min(bm, m)
    bn = min(bn, n)
    assert m % bm == 0 and n % bn == 0
    return pl.pallas_call(
        _mm_kernel,
        grid=(n // bn, m // bm),
        in_specs=[pl.BlockSpec((bm, k), lambda j, i: (i, 0)),
                  pl.BlockSpec((k, bn), lambda j, i: (0, j))],
        out_specs=pl.BlockSpec((bm, bn), lambda j, i: (i, j)),
        out_shape=jax.ShapeDtypeStruct((m, n), out_dtype),
        compiler_params=_cparams("parallel", "parallel"),
        name=name,
    )(x, w)


def _conv_epilogue(c, z, g_ref, beta_ref):
    mu = jnp.mean(c, axis=-1, keepdims=True)
    cc = c - mu
    var = jnp.mean(cc * cc, axis=-1, keepdims=True)
    ln = cc * lax.rsqrt(var + EPS) * g_ref[...] + beta_ref[...]
    return _silu(ln) * _silu(z)


def _conv_seq_kernel(a_ref, b_ref, z_ref, hist_ref, w_ref, cb_ref, g_ref, beta_ref,
                     y_ref, tail_ref, ext_ref, *, tt):
    i = pl.program_id(0)

    @pl.when(i == 0)
    def _():
        ext_ref[0:CONV_HALO, :] = hist_ref[...]

    @pl.when(i > 0)
    def _():
        ext_ref[0:CONV_HALO, :] = ext_ref[tt:tt + CONV_HALO, :]

    v = a_ref[...] * _sigmoid(b_ref[...])
    ext_ref[CONV_HALO:CONV_HALO + tt, :] = v
    base = CONV_HALO - CONV_HIST
    c = ext_ref[base:base + tt, :] * w_ref[0:1, :] + cb_ref[...]
    for k in range(1, CONV_WIDTH):
        c = c + ext_ref[base + k:base + k + tt, :] * w_ref[k:k + 1, :]
    y_ref[...] = _conv_epilogue(c, z_ref[...], g_ref, beta_ref).astype(y_ref.dtype)
    tail_ref[...] = ext_ref[tt:tt + CONV_HALO, :]


def conv_branch_seq(proj_a, hist, conv_w, conv_b, ln_g, ln_b, tt=256):
    t = proj_a.shape[0]
    c = CONV_DIM
    tt = min(tt, t)
    hist_p = jnp.concatenate([jnp.zeros((CONV_HALO - CONV_HIST, c), F32), hist], axis=0)
    w_p = jnp.concatenate([conv_w, jnp.zeros((CONV_HALO - CONV_WIDTH, c), F32)], axis=0)
    row = lambda a: a.reshape(1, c)
    const = lambda shape: pl.BlockSpec(shape, lambda i: (0, 0))
    y, tail = pl.pallas_call(
        functools.partial(_conv_seq_kernel, tt=tt),
        grid=(t // tt,),
        in_specs=[pl.BlockSpec((tt, c), lambda i: (i, 0)),
                  pl.BlockSpec((tt, c), lambda i: (i, 1)),
                  pl.BlockSpec((tt, c), lambda i: (i, 2)),
                  const((CONV_HALO, c)), const((CONV_HALO, c)),
                  const((1, c)), const((1, c)), const((1, c))],
        out_specs=[pl.BlockSpec((tt, c), lambda i: (i, 0)), const((CONV_HALO, c))],
        out_shape=[jax.ShapeDtypeStruct((t, c), BF16), jax.ShapeDtypeStruct((CONV_HALO, c), F32)],
        scratch_shapes=[pltpu.VMEM((tt + CONV_HALO, c), F32)],
        compiler_params=_cparams("arbitrary"),
        name="conv_seq",
    )(proj_a, proj_a, proj_a, hist_p, w_p, row(conv_b), row(ln_g), row(ln_b))
    return y, tail[CONV_HALO - CONV_HIST:]


def _conv_step_kernel(a_ref, b_ref, z_ref, st_ref, w_ref, cb_ref, g_ref, beta_ref, y_ref, new_ref):
    cdim = CONV_DIM
    v = a_ref[...] * _sigmoid(b_ref[...])
    c = v * w_ref[CONV_HIST:CONV_HIST + 1, :] + cb_ref[...]
    for k in range(CONV_HIST):
        c = c + st_ref[:, k * cdim:(k + 1) * cdim] * w_ref[k:k + 1, :]
    y_ref[...] = _conv_epilogue(c, z_ref[...], g_ref, beta_ref).astype(y_ref.dtype)
    new_ref[:, 0:(CONV_HIST - 1) * cdim] = st_ref[:, cdim:CONV_HIST * cdim]
    new_ref[:, (CONV_HIST - 1) * cdim:] = v


def conv_branch_step(proj_a, state, conv_w, conv_b, ln_g, ln_b, bb=16):
    b = proj_a.shape[0]
    c = CONV_DIM
    bb = min(bb, b)
    st2 = state.reshape(b, CONV_HIST * c)
    w_p = jnp.concatenate([conv_w, jnp.zeros((CONV_HALO - CONV_WIDTH, c), F32)], axis=0)
    row = lambda a: a.reshape(1, c)
    const = lambda shape: pl.BlockSpec(shape, lambda i: (0, 0))
    y, new = pl.pallas_call(
        _conv_step_kernel,
        grid=(b // bb,),
        in_specs=[pl.BlockSpec((bb, c), lambda i: (i, 0)),
                  pl.BlockSpec((bb, c), lambda i: (i, 1)),
                  pl.BlockSpec((bb, c), lambda i: (i, 2)),
                  pl.BlockSpec((bb, CONV_HIST * c), lambda i: (i, 0)),
                  const((CONV_HALO, c)), const((1, c)), const((1, c)), const((1, c))],
        out_specs=[pl.BlockSpec((bb, c), lambda i: (i, 0)),
                   pl.BlockSpec((bb, CONV_HIST * c), lambda i: (i, 0))],
        out_shape=[jax.ShapeDtypeStruct((b, c), BF16),
                   jax.ShapeDtypeStruct((b, CONV_HIST * c), F32)],
        compiler_params=_cparams("parallel"),
        name="conv_step",
    )(proj_a, proj_a, proj_a, st2, w_p, row(conv_b), row(ln_g), row(ln_b))
    return y, new.reshape(b, CONV_HIST, c)


def _pool_mix(s_of_group, u, z, pos, pw_ref, ps_ref, y_ref):
    gd = POOL_GROUP_DIM
    for g, w in enumerate(POOL_WINDOWS):
        sl = slice(g * gd, (g + 1) * gd)
        cnt = jnp.minimum(w, pos + 1).astype(F32)
        d = (s_of_group(g, w) / cnt - u[:, sl]).astype(BF16)
        mixed = _dot(d, pw_ref[g])
        y_ref[:, sl] = (mixed * ps_ref[:, sl] * _silu(z[:, sl])).astype(y_ref.dtype)


def _pool_seq_kernel(u_ref, z_ref, hist_ref, pw_ref, ps_ref, y_ref, tail_ref, ext_ref, *, tt, pos0):
    i = pl.program_id(0)
    gd = POOL_GROUP_DIM

    @pl.when(i == 0)
    def _():
        ext_ref[0:POOL_HALO, :] = hist_ref[...]

    @pl.when(i > 0)
    def _():
        ext_ref[0:POOL_HALO, :] = ext_ref[tt:tt + POOL_HALO, :]

    u = u_ref[...]
    ext_ref[POOL_HALO:POOL_HALO + tt, :] = u
    pos = pos0 + i * tt + lax.broadcasted_iota(jnp.int32, (tt, gd), 0)

    def window_sum(g, w):
        sl = slice(g * gd, (g + 1) * gd)
        s = ext_ref[POOL_HALO:POOL_HALO + tt, sl]
        for j in range(1, w):
            s = s + ext_ref[POOL_HALO - j:POOL_HALO - j + tt, sl]
        return s

    _pool_mix(window_sum, u, z_ref[...], pos, pw_ref, ps_ref, y_ref)
    tail_ref[...] = ext_ref[tt:tt + POOL_HALO, :]


def pool_branch_seq(proj_a, hist, pool_w_bf, pool_scale, pos0, tt=256):
    t = proj_a.shape[0]
    c = POOL_DIM
    tt = min(tt, t)
    hist_p = jnp.concatenate([jnp.zeros((POOL_HALO - POOL_HIST, c), F32), hist], axis=0)
    const2 = lambda shape: pl.BlockSpec(shape, lambda i: (0, 0))
    y, tail = pl.pallas_call(
        functools.partial(_pool_seq_kernel, tt=tt, pos0=pos0),
        grid=(t // tt,),
        in_specs=[pl.BlockSpec((tt, c), lambda i: (i, 3)),
                  pl.BlockSpec((tt, c), lambda i: (i, 4)),
                  const2((POOL_HALO, c)),
                  pl.BlockSpec(pool_w_bf.shape, lambda i: (0, 0, 0)),
                  const2((1, c))],
        out_specs=[pl.BlockSpec((tt, c), lambda i: (i, 0)), const2((POOL_HALO, c))],
        out_shape=[jax.ShapeDtypeStruct((t, c), BF16), jax.ShapeDtypeStruct((POOL_HALO, c), F32)],
        scratch_shapes=[pltpu.VMEM((tt + POOL_HALO, c), F32)],
        compiler_params=_cparams("arbitrary"),
        name="pool_seq",
    )(proj_a, proj_a, hist_p, pool_w_bf, pool_scale.reshape(1, c))
    return y, tail[POOL_HALO - POOL_HIST:]


def _pool_step_kernel(u_ref, z_ref, st_ref, pw_ref, ps_ref, y_ref, new_ref, *, pos0):
    cdim = POOL_DIM
    gd = POOL_GROUP_DIM
    u = u_ref[...]
    pos = jnp.full((u.shape[0], gd), pos0, jnp.int32)

    def window_sum(g, w):
        s = u[:, g * gd:(g + 1) * gd]
        for j in range(1, w):
            k = POOL_HIST - j
            s = s + st_ref[:, k * cdim + g * gd:k * cdim + (g + 1) * gd]
        return s

    _pool_mix(window_sum, u, z_ref[...], pos, pw_ref, ps_ref, y_ref)
    new_ref[:, 0:(POOL_HIST - 1) * cdim] = st_ref[:, cdim:POOL_HIST * cdim]
    new_ref[:, (POOL_HIST - 1) * cdim:] = u


def pool_branch_step(proj_a, state, pool_w_bf, pool_scale, pos0, bb=16):
    b = proj_a.shape[0]
    c = POOL_DIM
    bb = min(bb, b)
    st2 = state.reshape(b, POOL_HIST * c)
    y, new = pl.pallas_call(
        functools.partial(_pool_step_kernel, pos0=pos0),
        grid=(b // bb,),
        in_specs=[pl.BlockSpec((bb, c), lambda i: (i, 3)),
                  pl.BlockSpec((bb, c), lambda i: (i, 4)),
                  pl.BlockSpec((bb, POOL_HIST * c), lambda i: (i, 0)),
                  pl.BlockSpec(pool_w_bf.shape, lambda i: (0, 0, 0)),
                  pl.BlockSpec((1, c), lambda i: (0, 0))],
        out_specs=[pl.BlockSpec((bb, c), lambda i: (i, 0)),
                   pl.BlockSpec((bb, POOL_HIST * c), lambda i: (i, 0))],
        out_shape=[jax.ShapeDtypeStruct((b, c), BF16),
                   jax.ShapeDtypeStruct((b, POOL_HIST * c), F32)],
        compiler_params=_cparams("parallel"),
        name="pool_step",
    )(proj_a, proj_a, st2, pool_w_bf, pool_scale.reshape(1, c))
    return y, new.reshape(b, POOL_HIST, c)


def _rms(x, g):
    return x * lax.rsqrt(jnp.mean(x * x, axis=-1, keepdims=True) + EPS) * g


def _mla_prep_kernel(pb_ref, qg_ref, kvg_ref, ck_ref, sk_ref,
                     qn_ref, lat_ref, latb_ref, kr_ref, kadd_ref):
    q0, q1 = 0, Q_LORA_RANK
    k1 = q1 + KV_LORA_RANK
    r1 = k1 + QK_PAD
    r2 = r1 + QK_PAD
    qn_ref[...] = _rms(pb_ref[:, q0:q1], qg_ref[...]).astype(qn_ref.dtype)
    lat = _rms(pb_ref[:, q1:k1], kvg_ref[...])
    lat_ref[...] = lat
    latb_ref[...] = lat.astype(latb_ref.dtype)
    kadd = pb_ref[:, k1:r1] * ck_ref[...] + pb_ref[:, r1:r2] * sk_ref[...]
    kadd_ref[...] = kadd
    kr_ref[...] = kadd[:, ROPE_OFF:ROPE_OFF + QK_ROPE_DIM]


def mla_prep(proj_b, q_norm_g, kv_norm_g, ck, sk, tt=512):
    t = proj_b.shape[0]
    tt = min(tt, t)
    rowblk = lambda w: pl.BlockSpec((tt, w), lambda i: (i, 0))
    const = lambda w: pl.BlockSpec((1, w), lambda i: (0, 0))
    return pl.pallas_call(
        _mla_prep_kernel,
        grid=(t // tt,),
        in_specs=[rowblk(PROJ_B_DIM), const(Q_LORA_RANK), const(KV_LORA_RANK), rowblk(QK_PAD), rowblk(QK_PAD)],
        out_specs=[rowblk(Q_LORA_RANK), rowblk(KV_LORA_RANK), rowblk(KV_LORA_RANK),
                   rowblk(QK_ROPE_DIM), rowblk(QK_PAD)],
        out_shape=[jax.ShapeDtypeStruct((t, Q_LORA_RANK), BF16),
                   jax.ShapeDtypeStruct((t, KV_LORA_RANK), F32),
                   jax.ShapeDtypeStruct((t, KV_LORA_RANK), BF16),
                   jax.ShapeDtypeStruct((t, QK_ROPE_DIM), F32),
                   jax.ShapeDtypeStruct((t, QK_PAD), F32)],
        compiler_params=_cparams("parallel"),
        name="mla_prep",
    )(proj_b, q_norm_g.reshape(1, -1), kv_norm_g.reshape(1, -1), ck, sk)


def _qproj_kernel(x_ref, w1_ref, w2_ref, c_ref, s_ref, o_ref):
    x = x_ref[...]
    o_ref[0] = (_dot(x, w1_ref[0]) * c_ref[...] + _dot(x, w2_ref[0]) * s_ref[...]).astype(o_ref.dtype)


def q_proj(qn, wq1, wq2, cq, sq, bm=1024):
    t, r = qn.shape
    bm = min(bm, t)
    return pl.pallas_call(
        _qproj_kernel,
        grid=(N_HEADS, t // bm),
        in_specs=[pl.BlockSpec((bm, r), lambda h, i: (i, 0)),
                  pl.BlockSpec((1, r, QK_PAD), lambda h, i: (h, 0, 0)),
                  pl.BlockSpec((1, r, QK_PAD), lambda h, i: (h, 0, 0)),
                  pl.BlockSpec((bm, QK_PAD), lambda h, i: (i, 0)),
                  pl.BlockSpec((bm, QK_PAD), lambda h, i: (i, 0))],
        out_specs=pl.BlockSpec((1, bm, QK_PAD), lambda h, i: (h, i, 0)),
        out_shape=jax.ShapeDtypeStruct((N_HEADS, t, QK_PAD), BF16),
        compiler_params=_cparams("parallel", "parallel"),
        name="q_proj",
    )(qn, wq1, wq2, cq, sq)


def _kcat_kernel(x_ref, w_ref, add_ref, o_ref):
    o_ref[0] = (_dot(x_ref[...], w_ref[0]) + add_ref[...]).astype(o_ref.dtype)


def k_cat(latb, wuk_pad, kadd, bm=1024):
    t, r = latb.shape
    bm = min(bm, t)
    return pl.pallas_call(
        _kcat_kernel,
        grid=(t // bm, N_HEADS),
        in_specs=[pl.BlockSpec((bm, r), lambda i, h: (i, 0)),
                  pl.BlockSpec((1, r, QK_PAD), lambda i, h: (h, 0, 0)),
                  pl.BlockSpec((bm, QK_PAD), lambda i, h: (i, 0))],
        out_specs=pl.BlockSpec((1, bm, QK_PAD), lambda i, h: (h, i, 0)),
        out_shape=jax.ShapeDtypeStruct((N_HEADS, t, QK_PAD), BF16),
        compiler_params=_cparams("parallel", "parallel"),
        name="k_cat",
    )(latb, wuk_pad, kadd)


def _qproj_t_kernel(x_ref, w1_ref, w2_ref, c_ref, s_ref, o_ref):
    x = x_ref[...]
    o_ref[0] = (_dot_nt(w1_ref[0], x) * c_ref[...] + _dot_nt(w2_ref[0], x) * s_ref[...]).astype(o_ref.dtype)


def q_proj_t(qn, wq1_t, wq2_t, cq_t, sq_t, bm=1024):
    t, r = qn.shape
    bm = min(bm, t)
    return pl.pallas_call(
        _qproj_t_kernel,
        grid=(t // bm, N_HEADS),
        in_specs=[pl.BlockSpec((bm, r), lambda i, h: (i, 0)),
                  pl.BlockSpec((1, QK_PAD, r), lambda i, h: (h, 0, 0)),
                  pl.BlockSpec((1, QK_PAD, r), lambda i, h: (h, 0, 0)),
                  pl.BlockSpec((QK_PAD, bm), lambda i, h: (0, i)),
                  pl.BlockSpec((QK_PAD, bm), lambda i, h: (0, i))],
        out_specs=pl.BlockSpec((1, QK_PAD, bm), lambda i, h: (h, 0, i)),
        out_shape=jax.ShapeDtypeStruct((N_HEADS, QK_PAD, t), BF16),
        compiler_params=_cparams("parallel", "parallel"),
        name="q_proj_t",
    )(qn, wq1_t, wq2_t, cq_t, sq_t)


def _vproj_t_kernel(x_ref, w_ref, o_ref):
    o_ref[0] = _dot_nt(w_ref[0], x_ref[...]).astype(o_ref.dtype)


def v_proj_t(latb, wuv_t, bm=1024):
    t, r = latb.shape
    bm = min(bm, t)
    return pl.pallas_call(
        _vproj_t_kernel,
        grid=(t // bm, N_HEADS),
        in_specs=[pl.BlockSpec((bm, r), lambda i, h: (i, 0)),
                  pl.BlockSpec((1, V_HEAD_DIM, r), lambda i, h: (h, 0, 0))],
        out_specs=pl.BlockSpec((1, V_HEAD_DIM, bm), lambda i, h: (h, 0, i)),
        out_shape=jax.ShapeDtypeStruct((N_HEADS, V_HEAD_DIM, t), BF16),
        compiler_params=_cparams("parallel", "parallel"),
        name="v_proj_t",
    )(latb, wuv_t)


EXP2_SCALE = SOFTMAX_SCALE * 1.4426950408889634


def _flash_kernel(qt_ref, k_ref, vt_ref, z_ref, o_ref, sa_ref, sb_ref, *, blk):
    qi = pl.program_id(1)
    qt = qt_ref[0]

    def scores_into(j, s_ref):
        start = pl.multiple_of(j * blk, blk)
        s_ref[...] = _dot(k_ref[0, pl.ds(start, blk), :], qt)

    def update(j, s_ref, carry, masked=False):
        m, l, acc = carry
        s = s_ref[...]
        if masked:
            key = lax.broadcasted_iota(jnp.int32, (blk, blk), 0)
            qry = lax.broadcasted_iota(jnp.int32, (blk, blk), 1)
            s = jnp.where(key <= qry, s, -jnp.inf)
        m_new = jnp.maximum(m, jnp.max(s, axis=0, keepdims=True))
        alpha = jnp.exp2((m - m_new) * EXP2_SCALE)
        p = jnp.exp2((s - m_new) * EXP2_SCALE)
        l = l * alpha + jnp.sum(p, axis=0, keepdims=True)
        start = pl.multiple_of(j * blk, blk)
        vt = vt_ref[0, :, pl.ds(start, blk)]
        acc = acc * alpha + _dot(vt, p.astype(BF16))
        return m_new, l, acc

    def finish(carry):
        _, l, acc = carry
        o_ref[...] = ((acc / l).T * _silu(z_ref[...])).astype(o_ref.dtype)

    scores_into(0, sa_ref)

    def pair(t, carry):
        scores_into(2 * t + 1, sb_ref)
        carry = update(2 * t, sa_ref, carry)
        scores_into(2 * t + 2, sa_ref)
        return update(2 * t + 1, sb_ref, carry)

    init = (jnp.full((1, blk), NEG_INIT, F32), jnp.zeros((1, blk), F32), jnp.zeros((V_HEAD_DIM, blk), F32))
    carry = lax.fori_loop(0, qi // 2, pair, init)

    @pl.when(qi % 2 == 0)
    def _():
        finish(update(qi, sa_ref, carry, masked=True))

    @pl.when(qi % 2 == 1)
    def _():
        scores_into(qi, sb_ref)
        finish(update(qi, sb_ref, update(qi - 1, sa_ref, carry), masked=True))


def flash_attention(qt, kc, vt, proj_c, blk=512):
    t = kc.shape[1]
    blk = min(blk, t)
    return pl.pallas_call(
        functools.partial(_flash_kernel, blk=blk),
        grid=(N_HEADS, t // blk),
        in_specs=[pl.BlockSpec((1, QK_PAD, blk), lambda h, i: (h, 0, i)),
                  pl.BlockSpec((1, t, QK_PAD), lambda h, i: (h, 0, 0)),
                  pl.BlockSpec((1, V_HEAD_DIM, t), lambda h, i: (h, 0, 0)),
                  pl.BlockSpec((blk, V_HEAD_DIM), lambda h, i: (i, h))],
        out_specs=pl.BlockSpec((blk, V_HEAD_DIM), lambda h, i: (i, h)),
        out_shape=jax.ShapeDtypeStruct((t, ATTN_DIM), BF16),
        scratch_shapes=[pltpu.VMEM((blk, blk), F32), pltpu.VMEM((blk, blk), F32)],
        compiler_params=_cparams("parallel", "arbitrary"),
        name="flash_attention",
    )(qt, kc, vt, proj_c)


def _qlat_kernel(qc_ref, w_ref, o_ref):
    o_ref[0] = _dot(qc_ref[0][:, 0:QK_NOPE_DIM], w_ref[0]).astype(o_ref.dtype)


def q_latent(qc, wuk_t):
    b = qc.shape[1]
    return pl.pallas_call(
        _qlat_kernel,
        grid=(N_HEADS,),
        in_specs=[pl.BlockSpec((1, b, QK_PAD), lambda h: (h, 0, 0)),
                  pl.BlockSpec((1, QK_NOPE_DIM, KV_LORA_RANK), lambda h: (h, 0, 0))],
        out_specs=pl.BlockSpec((1, b, KV_LORA_RANK), lambda h: (h, 0, 0)),
        out_shape=jax.ShapeDtypeStruct((N_HEADS, b, KV_LORA_RANK), BF16),
        compiler_params=_cparams("parallel"),
        name="q_latent",
    )(qc, wuk_t)


def _decode_kernel(pt_ref, ql_ref, qr_ref, sl_ref, sr_ref, ckv_hbm, kr_hbm, o_ref,
                   lat_buf, rope_buf, sem, m_sc, l_sc, acc_sc, kbuf, rbuf,
                   *, layer, n_pg, n_ch, steps_per_row, total_steps):
    cp = n_pg // n_ch
    i = pl.program_id(0)
    ql = ql_ref[0]
    qr = qr_ref[0]

    def page_copies(step, slot):
        row = step // steps_per_row
        first = (step % steps_per_row) * n_pg
        copies = []
        for p in range(n_pg):
            page = pt_ref[row, first + p]
            copies.append(pltpu.make_async_copy(ckv_hbm.at[layer, page], lat_buf.at[slot, p], sem.at[slot, 0]))
            copies.append(pltpu.make_async_copy(kr_hbm.at[layer, page], rope_buf.at[slot, p], sem.at[slot, 1]))
        return copies

    def start(step, slot):
        for cpy in page_copies(step, slot):
            cpy.start()

    def wait(step, slot):
        for cpy in page_copies(step, slot):
            cpy.wait()

    def compute(slot, j):
        @pl.when(j == 0)
        def _():
            sl = sl_ref[0].astype(BF16).astype(F32)
            sr = sr_ref[0].astype(BF16).astype(F32)
            s0 = (jnp.sum(ql.astype(F32) * sl, axis=-1, keepdims=True)
                  + jnp.sum(qr.astype(F32) * sr, axis=-1, keepdims=True)) * SOFTMAX_SCALE
            m_sc[...] = jnp.maximum(s0, NEG_INIT)
            l_sc[...] = jnp.ones_like(l_sc)
            acc_sc[...] = jnp.broadcast_to(sl, acc_sc.shape)

        parts = []
        for c in range(n_ch):
            for p in range(cp):
                kbuf[c, p * PAGE_SIZE:(p + 1) * PAGE_SIZE, :] = lat_buf[slot, c * cp + p].astype(BF16)
                rbuf[c, :, p * PAGE_SIZE:(p + 1) * PAGE_SIZE] = rope_buf[slot, c * cp + p].astype(BF16)
            k = kbuf[c]
            s = (_dot_nt(ql, k) + _dot(qr, rbuf[c])) * SOFTMAX_SCALE
            m_c = jnp.max(s, axis=-1, keepdims=True)
            p_c = jnp.exp(s - m_c)
            parts.append((m_c, jnp.sum(p_c, axis=-1, keepdims=True), _dot(p_c.astype(BF16), k)))
        m_prev = m_sc[...]
        m_new = m_prev
        for m_c, _, _ in parts:
            m_new = jnp.maximum(m_new, m_c)
        alpha = jnp.exp(m_prev - m_new)
        l = l_sc[...] * alpha
        acc = acc_sc[...] * alpha
        for m_c, l_c, acc_c in parts:
            w = jnp.exp(m_c - m_new)
            l = l + l_c * w
            acc = acc + acc_c * w
        l_sc[...] = l
        acc_sc[...] = acc
        m_sc[...] = m_new

    s_even = 2 * i
    s_odd = s_even + 1
    s_next = jnp.minimum(s_even + 2, total_steps - 1)
    j_even = s_even % steps_per_row

    @pl.when(i == 0)
    def _():
        start(s_even, 0)

    wait(s_even, 0)
    start(s_odd, 1)
    compute(0, j_even)
    wait(s_odd, 1)
    start(s_next, 0)
    compute(1, j_even + 1)

    @pl.when(j_even + 2 == steps_per_row)
    def _():
        o_ref[0] = acc_sc[...] / l_sc[...]

    @pl.when(i == pl.num_programs(0) - 1)
    def _():
        wait(s_next, 0)


def decode_attention(q_lat, q_rope, lat_new, kr_new, ckv_cache, kr_cache_t, page_table, layer):
    b, n_pages = page_table.shape
    n_pg = next(n for n in (DECODE_PAGES_PER_STEP, 4, 1) if n_pages % (2 * n) == 0)
    n_ch = min(DECODE_CHUNKS_PER_STEP, n_pg)
    cp = n_pg // n_ch
    steps_per_row = n_pages // n_pg
    iters_per_row = steps_per_row // 2

    per_row = lambda d1, d2: pl.BlockSpec((1, d1, d2), lambda i, pt: (i // iters_per_row, 0, 0))
    hbm = pl.BlockSpec(memory_space=pl.ANY)
    grid_spec = pltpu.PrefetchScalarGridSpec(
        num_scalar_prefetch=1,
        grid=(b * iters_per_row,),
        in_specs=[per_row(N_HEADS, KV_LORA_RANK), per_row(N_HEADS, QK_ROPE_DIM),
                  per_row(1, KV_LORA_RANK), per_row(1, QK_ROPE_DIM), hbm, hbm],
        out_specs=per_row(N_HEADS, KV_LORA_RANK),
        scratch_shapes=[pltpu.VMEM((2, n_pg, PAGE_SIZE, KV_LORA_RANK), F32),
                        pltpu.VMEM((2, n_pg, QK_ROPE_DIM, PAGE_SIZE), F32),
                        pltpu.SemaphoreType.DMA((2, 2)),
                        pltpu.VMEM((N_HEADS, 1), F32), pltpu.VMEM((N_HEADS, 1), F32),
                        pltpu.VMEM((N_HEADS, KV_LORA_RANK), F32),
                        pltpu.VMEM((n_ch, cp * PAGE_SIZE, KV_LORA_RANK), BF16),
                        pltpu.VMEM((n_ch, QK_ROPE_DIM, cp * PAGE_SIZE), BF16)],
    )
    return pl.pallas_call(
        functools.partial(_decode_kernel, layer=layer, n_pg=n_pg, n_ch=n_ch,
                          steps_per_row=steps_per_row, total_steps=b * steps_per_row),
        grid_spec=grid_spec,
        out_shape=jax.ShapeDtypeStruct((b, N_HEADS, KV_LORA_RANK), F32),
        compiler_params=_cparams("arbitrary"),
        name="decode_attention",
    )(page_table, q_lat, q_rope, lat_new.reshape(b, 1, -1), kr_new.reshape(b, 1, -1),
      ckv_cache, kr_cache_t)


def _oproj_kernel(ol_ref, w_ref, z_ref, o_ref):
    o = _dot(ol_ref[0].astype(BF16), w_ref[0])
    o_ref[...] = (o * _silu(z_ref[...])).astype(o_ref.dtype)


def o_proj(o_lat_h, wuv_h, proj_c):
    b = o_lat_h.shape[1]
    return pl.pallas_call(
        _oproj_kernel,
        grid=(N_HEADS,),
        in_specs=[pl.BlockSpec((1, b, KV_LORA_RANK), lambda h: (h, 0, 0)),
                  pl.BlockSpec((1, KV_LORA_RANK, V_HEAD_DIM), lambda h: (h, 0, 0)),
                  pl.BlockSpec((b, V_HEAD_DIM), lambda h: (0, h))],
        out_specs=pl.BlockSpec((b, V_HEAD_DIM), lambda h: (0, h)),
        out_shape=jax.ShapeDtypeStruct((b, ATTN_DIM), BF16),
        compiler_params=_cparams("parallel"),
        name="o_proj",
    )(o_lat_h, wuv_h, proj_c)


def _merge_kernel(yc_ref, yp_ref, ya_ref, wc_ref, wp_ref, wa_ref, g0_ref, g1_ref, g2_ref, o_ref):
    merged = (_sigmoid(g0_ref[...]) * _dot(yc_ref[...], wc_ref[...])
              + _sigmoid(g1_ref[...]) * _dot(yp_ref[...], wp_ref[...])
              + _sigmoid(g2_ref[...]) * _dot(ya_ref[...], wa_ref[...]))
    o_ref[...] = merged.astype(o_ref.dtype)


def gated_merge(yc, yp, ya, wc, wp, wa, proj_c, bm=1024, bn=512):
    t = yc.shape[0]
    d = D_MODEL
    bm = min(bm, t)
    gate_blk0 = ATTN_DIM // bn
    per_branch = d // bn
    act = lambda w: pl.BlockSpec((bm, w), lambda j, i: (i, 0))
    wsp = lambda k: pl.BlockSpec((k, bn), lambda j, i: (0, j))
    gate = lambda br: pl.BlockSpec((bm, bn), lambda j, i: (i, gate_blk0 + br * per_branch + j))
    return pl.pallas_call(
        _merge_kernel,
        grid=(d // bn, t // bm),
        in_specs=[act(CONV_DIM), act(POOL_DIM), act(ATTN_DIM),
                  wsp(CONV_DIM), wsp(POOL_DIM), wsp(ATTN_DIM),
                  gate(0), gate(1), gate(2)],
        out_specs=pl.BlockSpec((bm, bn), lambda j, i: (i, j)),
        out_shape=jax.ShapeDtypeStruct((t, d), BF16),
        compiler_params=_cparams("parallel", "parallel"),
        name="gated_merge",
    )(yc, yp, ya, wc, wp, wa, proj_c, proj_c, proj_c)


def _out_kernel(m_ref, w_ref, x_ref, o_ref):
    o_ref[...] = x_ref[...] + _dot(m_ref[...], w_ref[...])


def out_proj_residual(merged, w_out, x, bm=1024, bn=1024):
    t, d = x.shape
    bm = min(bm, t)
    return pl.pallas_call(
        _out_kernel,
        grid=(d // bn, t // bm),
        in_specs=[pl.BlockSpec((bm, d), lambda j, i: (i, 0)),
                  pl.BlockSpec((d, bn), lambda j, i: (0, j)),
                  pl.BlockSpec((bm, bn), lambda j, i: (i, j))],
        out_specs=pl.BlockSpec((bm, bn), lambda j, i: (i, j)),
        out_shape=jax.ShapeDtypeStruct((t, d), F32),
        compiler_params=_cparams("parallel", "parallel"),
        name="out_proj",
    )(merged, w_out, x)


def _swap_halves(w):
    half = QK_ROPE_DIM // 2
    return jnp.concatenate([w[..., half:], w[..., :half]], axis=-1)


def _prep_layer_weights(w_in, w_q_up, w_uk, w_uv, pool_w, w_conv_out, w_pool_out, w_attn_out, w_out):
    d = w_in.shape[0]
    kr = w_in[:, OFF_KR:OFF_KR + QK_ROPE_DIM]
    zl = jnp.zeros((d, ROPE_OFF), F32)
    zr = jnp.zeros((d, QK_PAD - ROPE_OFF - QK_ROPE_DIM), F32)
    w_b = jnp.concatenate([w_in[:, OFF_CQ:OFF_KR], zl, kr, zr, zl, _swap_halves(kr), zr], axis=1)
    r = w_q_up.shape[0]
    wq = w_q_up.reshape(r, N_HEADS, QK_NOPE_DIM + QK_ROPE_DIM).transpose(1, 0, 2)
    wq_nope, wq_rope = wq[..., :QK_NOPE_DIM], wq[..., QK_NOPE_DIM:]
    zq = jnp.zeros((N_HEADS, r, QK_PAD - ROPE_OFF - QK_ROPE_DIM), F32)
    wq1 = jnp.concatenate([wq_nope, wq_rope, zq], axis=-1)
    wq2 = jnp.concatenate([jnp.zeros_like(wq_nope), _swap_halves(wq_rope), zq], axis=-1)
    wuk_h = w_uk.transpose(1, 0, 2)
    wuk_pad = jnp.concatenate([wuk_h, jnp.zeros((N_HEADS, KV_LORA_RANK, QK_PAD - QK_NOPE_DIM), F32)], axis=-1)
    bf = lambda a: a.astype(BF16)
    return dict(
        w_a=bf(w_in[:, :PROJ_A_DIM]), w_b=bf(w_b), w_c=bf(w_in[:, OFF_ZATTN:]),
        wq1=bf(wq1), wq2=bf(wq2), wuk_pad=bf(wuk_pad),
        wq1_t=bf(wq1.transpose(0, 2, 1)), wq2_t=bf(wq2.transpose(0, 2, 1)),
        wuk_t=bf(w_uk.transpose(1, 2, 0)),
        wuv_t=bf(w_uv.transpose(1, 2, 0)),
        wuv_h=bf(w_uv.transpose(1, 0, 2)),
        pool_w=bf(pool_w), w_conv_out=bf(w_conv_out), w_pool_out=bf(w_pool_out),
        w_attn_out=bf(w_attn_out), w_out=bf(w_out),
    )


def _rope_tables(pos):
    half = QK_ROPE_DIM // 2
    freqs = ROPE_THETA ** (-jnp.arange(half, dtype=F32) / half)
    ang = pos.astype(F32)[:, None] * freqs[None, :]
    cos, sin = jnp.cos(ang), jnp.sin(ang)
    t = pos.shape[0]
    zl = jnp.zeros((t, ROPE_OFF), F32)
    zr = jnp.zeros((t, QK_PAD - ROPE_OFF - QK_ROPE_DIM), F32)
    ck = jnp.concatenate([zl, cos, cos, zr], axis=1)
    sk = jnp.concatenate([zl, -sin, sin, zr], axis=1)
    cq = jnp.concatenate([jnp.ones((t, ROPE_OFF), F32), cos, cos, zr], axis=1)
    return ck, sk, cq


def _layer_common(x, wts, norm_g, q_norm_g, kv_norm_g, tables):
    ck, sk, cq = tables
    h = rmsnorm(x, norm_g, BF16)
    proj_a = matmul(h, wts["w_a"], F32, name="proj_a")
    proj_b = matmul(h, wts["w_b"], F32, bn=PROJ_B_DIM, name="proj_b")
    proj_c = matmul(h, wts["w_c"], F32, name="proj_c")
    qn, lat, latb, kr, kadd = mla_prep(proj_b, q_norm_g, kv_norm_g, ck, sk)
    return proj_a, proj_c, qn, lat, latb, kr, kadd


def _layer_finish(x, y_conv, y_pool, y_attn, proj_c, wts):
    merged = gated_merge(y_conv, y_pool, y_attn, wts["w_conv_out"], wts["w_pool_out"], wts["w_attn_out"], proj_c)
    return out_proj_residual(merged, wts["w_out"], x)


def _prompt_layer(x, wts, p, tables):
    proj_a, proj_c, qn, lat, latb, kr, kadd = _layer_common(x, wts, p["norm_g"], p["q_norm_g"], p["kv_norm_g"], tables)
    y_conv, new_conv = conv_branch_seq(proj_a, jnp.zeros((CONV_HIST, CONV_DIM), F32),
                                       p["conv_w"], p["conv_b"], p["conv_ln_g"], p["conv_ln_b"])
    y_pool, new_pool = pool_branch_seq(proj_a, jnp.zeros((POOL_HIST, POOL_DIM), F32),
                                       wts["pool_w"], p["pool_scale"], 0)
    _, sk, cq = tables
    qt = q_proj_t(qn, wts["wq1_t"], wts["wq2_t"], cq.T, sk.T)
    kc = k_cat(latb, wts["wuk_pad"], kadd)
    vt = v_proj_t(latb, wts["wuv_t"])
    y_attn = flash_attention(qt, kc, vt, proj_c)
    x_new = _layer_finish(x, y_conv, y_pool, y_attn, proj_c, wts)
    return x_new, new_conv, new_pool, lat, kr


def _sample_layer(x, wts, p, tables, state_conv, state_pool, ckv_cache, kr_cache_t, page_table, layer):
    proj_a, proj_c, qn, lat, latb, kr, kadd = _layer_common(x, wts, p["norm_g"], p["q_norm_g"], p["kv_norm_g"], tables)
    y_conv, new_conv = conv_branch_step(proj_a, state_conv, p["conv_w"], p["conv_b"], p["conv_ln_g"], p["conv_ln_b"])
    y_pool, new_pool = pool_branch_step(proj_a, state_pool, wts["pool_w"], p["pool_scale"], PAST_LEN)
    _, sk, cq = tables
    qc = q_proj(qn, wts["wq1"], wts["wq2"], cq, sk)
    q_lat = q_latent(qc, wts["wuk_t"]).transpose(1, 0, 2)
    q_rope = qc[:, :, ROPE_OFF:ROPE_OFF + QK_ROPE_DIM].transpose(1, 0, 2)
    o_lat = decode_attention(q_lat, q_rope, lat, kr, ckv_cache, kr_cache_t, page_table, layer)
    y_attn = o_proj(o_lat.transpose(1, 0, 2), wts["wuv_h"], proj_c)
    x_new = _layer_finish(x, y_conv, y_pool, y_attn, proj_c, wts)
    return x_new, new_conv, new_pool, lat, kr


def kernel(x_prompt, x_sample, state_conv, state_pool, cache_kv_latent, cache_k_rope, page_table, norm_g, w_in, conv_w, conv_b, conv_ln_g, conv_ln_b, w_conv_out, pool_w, pool_scale, w_pool_out, q_norm_g, w_q_up, kv_norm_g, w_uk, w_uv, w_attn_out, w_out, final_norm_g):
    n_p, t_p, d = x_prompt.shape
    n_s, t_s, _ = x_sample.shape
    assert n_p == 1 and t_s == 1
    depth = w_in.shape[0]
    tables_p = _rope_tables(jnp.arange(t_p, dtype=jnp.int32))
    tables_s = _rope_tables(jnp.full((n_s,), PAST_LEN, jnp.int32))
    yp = x_prompt.reshape(t_p, d)
    ys = x_sample.reshape(n_s, d)
    kr_cache_t = jnp.swapaxes(cache_k_rope, 2, 3)
    outs_p, outs_s = [], []
    for l in range(depth):
        wts = _prep_layer_weights(w_in[l], w_q_up[l], w_uk[l], w_uv[l], pool_w[l],
                                  w_conv_out[l], w_pool_out[l], w_attn_out[l], w_out[l])
        p = dict(norm_g=norm_g[l], q_norm_g=q_norm_g[l], kv_norm_g=kv_norm_g[l],
                 conv_w=conv_w[l], conv_b=conv_b[l], conv_ln_g=conv_ln_g[l], conv_ln_b=conv_ln_b[l],
                 pool_scale=pool_scale[l])
        ys, *rest_s = _sample_layer(ys, wts, p, tables_s, state_conv[l], state_pool[l],
                                    cache_kv_latent, kr_cache_t, page_table, l)
        yp, *rest_p = _prompt_layer(yp, wts, p, tables_p)
        outs_p.append(rest_p)
        outs_s.append(rest_s)
    y_prompt = rmsnorm(yp, final_norm_g, F32).reshape(n_p, t_p, d)
    y_sample = rmsnorm(ys, final_norm_g, F32).reshape(n_s, t_s, d)
    stack_p = lambda k: jnp.stack([o[k] for o in outs_p])[:, None]
    stack_s = lambda k: jnp.stack([o[k] for o in outs_s])
    return (y_prompt, y_sample,
            stack_p(0), stack_p(1), stack_p(2), stack_p(3),
            stack_s(0), stack_s(1), stack_s(2)[:, :, None, :], stack_s(3)[:, :, None, :])
```

```python
import functools

import jax
import jax.numpy as jnp
from jax import lax
from jax.experimental import pallas as pl
from jax.experimental.pallas import tpu as pltpu

D_MODEL = 2048
DEPTH = 2
PAST_LEN = 16384
PAGE_SIZE = 128
CONV_DIM = 1024
CONV_WIDTH = 31
CONV_HIST = CONV_WIDTH - 1
POOL_DIM = 1024
POOL_WINDOWS = (2, 4, 8, 16)
POOL_GROUP_DIM = POOL_DIM // len(POOL_WINDOWS)
POOL_HIST = max(POOL_WINDOWS) - 1
N_HEADS = 16
QK_NOPE_DIM = 128
QK_ROPE_DIM = 64
V_HEAD_DIM = 128
Q_LORA_RANK = 512
KV_LORA_RANK = 512
ATTN_DIM = N_HEADS * V_HEAD_DIM
ROPE_THETA = 10000.0
SOFTMAX_SCALE = (QK_NOPE_DIM + QK_ROPE_DIM) ** -0.5
NEG_INIT = -1e30
N_BRANCHES = 3
EPS = 1e-6

QK_PAD = 256
ROPE_OFF = QK_NOPE_DIM

OFF_CQ = 2 * CONV_DIM + CONV_DIM + 2 * POOL_DIM
OFF_CKV = OFF_CQ + Q_LORA_RANK
OFF_KR = OFF_CKV + KV_LORA_RANK
OFF_ZATTN = OFF_KR + QK_ROPE_DIM
PROJ_A_DIM = OFF_CQ
PROJ_B_DIM = Q_LORA_RANK + KV_LORA_RANK + 2 * QK_PAD
PROJ_C_DIM = ATTN_DIM + N_BRANCHES * D_MODEL

SUBLANES = 8
CONV_HALO = 32
POOL_HALO = 16

VMEM_LIMIT_MIB = 48

DECODE_PAGES_PER_STEP = 32
DECODE_CHUNKS_PER_STEP = 2
DECODE_SLOTS = 3

F32 = jnp.float32
BF16 = jnp.bfloat16


def _cparams(*sem):
    return pltpu.CompilerParams(dimension_semantics=sem, vmem_limit_bytes=VMEM_LIMIT_MIB << 20)


def _sigmoid(x):
    return 1.0 / (1.0 + jnp.exp(-x))


def _silu(x):
    return x * _sigmoid(x)


def _dot(a, b):
    return jnp.dot(a, b, preferred_element_type=F32)


def _dot_nt(a, b):
    return lax.dot_general(a, b, (((1,), (1,)), ((), ())), preferred_element_type=F32)


def _rmsnorm_kernel(x_ref, g_ref, o_ref):
    x = x_ref[...]
    y = x * lax.rsqrt(jnp.mean(x * x, axis=-1, keepdims=True) + EPS)
    o_ref[...] = (y * g_ref[...]).astype(o_ref.dtype)


def rmsnorm(x, g, out_dtype, bm=512):
    t, d = x.shape
    bm = min(bm, t)
    return pl.pallas_call(
        _rmsnorm_kernel,
        grid=(t // bm,),
        in_specs=[pl.BlockSpec((bm, d), lambda i: (i, 0)),
                  pl.BlockSpec((1, d), lambda i: (0, 0))],
        out_specs=pl.BlockSpec((bm, d), lambda i: (i, 0)),
        out_shape=jax.ShapeDtypeStruct((t, d), out_dtype),
        compiler_params=_cparams("parallel"),
        name="rmsnorm",
    )(x, g.reshape(1, d))


def _mm_kernel(x_ref, w_ref, o_ref):
    o_ref[...] = _dot(x_ref[...], w_ref[...]).astype(o_ref.dtype)


def matmul(x, w, out_dtype, bm=1024, bn=1024, name="matmul"):
    m, k = x.shape
    n = w.shape[1]
    bm = min(bm, m)
    bn = min(bn, n)
    assert m % bm == 0 and n % bn == 0
    return pl.pallas_call(
        _mm_kernel,
        grid=(n // bn, m // bm),
        in_specs=[pl.BlockSpec((bm, k), lambda j, i: (i, 0)),
                  pl.BlockSpec((k, bn), lambda j, i: (0, j))],
        out_specs=pl.BlockSpec((bm, bn), lambda j, i: (i, j)),
        out_shape=jax.ShapeDtypeStruct((m, n), out_dtype),
        compiler_params=_cparams("parallel", "parallel"),
        name=name,
    )(x, w)


def _conv_epilogue(c, z, g_ref, beta_ref):
    mu = jnp.mean(c, axis=-1, keepdims=True)
    cc = c - mu
    var = jnp.mean(cc * cc, axis=-1, keepdims=True)
    ln = cc * lax.rsqrt(var + EPS) * g_ref[...] + beta_ref[...]
    return _silu(ln) * _silu(z)


def _conv_seq_kernel(a_ref, b_ref, z_ref, hist_ref, w_ref, cb_ref, g_ref, beta_ref,
                     y_ref, tail_ref, ext_ref, sh_ref, *, tt):
    i = pl.program_id(0)
    rows = tt + CONV_HALO

    @pl.when(i == 0)
    def _():
        ext_ref[0:CONV_HALO, :] = hist_ref[...]

    @pl.when(i > 0)
    def _():
        ext_ref[0:CONV_HALO, :] = ext_ref[tt:tt + CONV_HALO, :]

    v = a_ref[...] * _sigmoid(b_ref[...])
    ext_ref[CONV_HALO:CONV_HALO + tt, :] = v
    for s in range(1, SUBLANES):
        sh_ref[s, 0:rows - s, :] = ext_ref[s:rows, :]
    base = CONV_HALO - CONV_HIST
    c = cb_ref[...]
    for k in range(CONV_WIDTH):
        s = (base + k) % SUBLANES
        a = base + k - s
        win = ext_ref[a:a + tt, :] if s == 0 else sh_ref[s, a:a + tt, :]
        c = c + win * w_ref[k:k + 1, :]
    y_ref[...] = _conv_epilogue(c, z_ref[...], g_ref, beta_ref).astype(y_ref.dtype)
    tail_ref[...] = ext_ref[tt:tt + CONV_HALO, :]


def conv_branch_seq(proj_a, hist, conv_w, conv_b, ln_g, ln_b, tt=256):
    t = proj_a.shape[0]
    c = CONV_DIM
    tt = min(tt, t)
    hist_p = jnp.concatenate([jnp.zeros((CONV_HALO - CONV_HIST, c), F32), hist], axis=0)
    w_p = jnp.concatenate([conv_w, jnp.zeros((CONV_HALO - CONV_WIDTH, c), F32)], axis=0)
    row = lambda a: a.reshape(1, c)
    const = lambda shape: pl.BlockSpec(shape, lambda i: (0, 0))
    y, tail = pl.pallas_call(
        functools.partial(_conv_seq_kernel, tt=tt),
        grid=(t // tt,),
        in_specs=[pl.BlockSpec((tt, c), lambda i: (i, 0)),
                  pl.BlockSpec((tt, c), lambda i: (i, 1)),
                  pl.BlockSpec((tt, c), lambda i: (i, 2)),
                  const((CONV_HALO, c)), const((CONV_HALO, c)),
                  const((1, c)), const((1, c)), const((1, c))],
        out_specs=[pl.BlockSpec((tt, c), lambda i: (i, 0)), const((CONV_HALO, c))],
        out_shape=[jax.ShapeDtypeStruct((t, c), BF16), jax.ShapeDtypeStruct((CONV_HALO, c), F32)],
        scratch_shapes=[pltpu.VMEM((tt + CONV_HALO, c), F32),
                        pltpu.VMEM((SUBLANES, tt + CONV_HALO, c), F32)],
        compiler_params=_cparams("arbitrary"),
        name="conv_seq",
    )(proj_a, proj_a, proj_a, hist_p, w_p, row(conv_b), row(ln_g), row(ln_b))
    return y, tail[CONV_HALO - CONV_HIST:]


def _conv_step_kernel(a_ref, b_ref, z_ref, st_ref, w_ref, cb_ref, g_ref, beta_ref, y_ref, new_ref):
    cdim = CONV_DIM
    v = a_ref[...] * _sigmoid(b_ref[...])
    c = v * w_ref[CONV_HIST:CONV_HIST + 1, :] + cb_ref[...]
    for k in range(CONV_HIST):
        c = c + st_ref[:, k * cdim:(k + 1) * cdim] * w_ref[k:k + 1, :]
    y_ref[...] = _conv_epilogue(c, z_ref[...], g_ref, beta_ref).astype(y_ref.dtype)
    new_ref[:, 0:(CONV_HIST - 1) * cdim] = st_ref[:, cdim:CONV_HIST * cdim]
    new_ref[:, (CONV_HIST - 1) * cdim:] = v


def conv_branch_step(proj_a, state, conv_w, conv_b, ln_g, ln_b, bb=16):
    b = proj_a.shape[0]
    c = CONV_DIM
    bb = min(bb, b)
    st2 = state.reshape(b, CONV_HIST * c)
    w_p = jnp.concatenate([conv_w, jnp.zeros((CONV_HALO - CONV_WIDTH, c), F32)], axis=0)
    row = lambda a: a.reshape(1, c)
    const = lambda shape: pl.BlockSpec(shape, lambda i: (0, 0))
    y, new = pl.pallas_call(
        _conv_step_kernel,
        grid=(b // bb,),
        in_specs=[pl.BlockSpec((bb, c), lambda i: (i, 0)),
                  pl.BlockSpec((bb, c), lambda i: (i, 1)),
                  pl.BlockSpec((bb, c), lambda i: (i, 2)),
                  pl.BlockSpec((bb, CONV_HIST * c), lambda i: (i, 0)),
                  const((CONV_HALO, c)), const((1, c)), const((1, c)), const((1, c))],
        out_specs=[pl.BlockSpec((bb, c), lambda i: (i, 0)),
                   pl.BlockSpec((bb, CONV_HIST * c), lambda i: (i, 0))],
        out_shape=[jax.ShapeDtypeStruct((b, c), BF16),
                   jax.ShapeDtypeStruct((b, CONV_HIST * c), F32)],
        compiler_params=_cparams("parallel"),
        name="conv_step",
    )(proj_a, proj_a, proj_a, st2, w_p, row(conv_b), row(ln_g), row(ln_b))
    return y, new.reshape(b, CONV_HIST, c)


def _pool_mix(s_of_group, u, z, pos, pw_ref, ps_ref, y_ref):
    gd = POOL_GROUP_DIM
    for g, w in enumerate(POOL_WINDOWS):
        sl = slice(g * gd, (g + 1) * gd)
        cnt = jnp.minimum(w, pos + 1).astype(F32)
        d = (s_of_group(g, w) / cnt - u[:, sl]).astype(BF16)
        mixed = _dot(d, pw_ref[g])
        y_ref[:, sl] = (mixed * ps_ref[:, sl] * _silu(z[:, sl])).astype(y_ref.dtype)


def _pool_seq_kernel(u_ref, z_ref, hist_ref, pw_ref, ps_ref, y_ref, tail_ref, ext_ref, *, tt, pos0):
    i = pl.program_id(0)
    gd = POOL_GROUP_DIM

    @pl.when(i == 0)
    def _():
        ext_ref[0:POOL_HALO, :] = hist_ref[...]

    @pl.when(i > 0)
    def _():
        ext_ref[0:POOL_HALO, :] = ext_ref[tt:tt + POOL_HALO, :]

    u = u_ref[...]
    ext_ref[POOL_HALO:POOL_HALO + tt, :] = u
    pos = pos0 + i * tt + lax.broadcasted_iota(jnp.int32, (tt, gd), 0)

    def window_sum(g, w):
        sl = slice(g * gd, (g + 1) * gd)
        s = ext_ref[POOL_HALO:POOL_HALO + tt, sl]
        for j in range(1, w):
            s = s + ext_ref[POOL_HALO - j:POOL_HALO - j + tt, sl]
        return s

    _pool_mix(window_sum, u, z_ref[...], pos, pw_ref, ps_ref, y_ref)
    tail_ref[...] = ext_ref[tt:tt + POOL_HALO, :]


def pool_branch_seq(proj_a, hist, pool_w_bf, pool_scale, pos0, tt=256):
    t = proj_a.shape[0]
    c = POOL_DIM
    tt = min(tt, t)
    hist_p = jnp.concatenate([jnp.zeros((POOL_HALO - POOL_HIST, c), F32), hist], axis=0)
    const2 = lambda shape: pl.BlockSpec(shape, lambda i: (0, 0))
    y, tail = pl.pallas_call(
        functools.partial(_pool_seq_kernel, tt=tt, pos0=pos0),
        grid=(t // tt,),
        in_specs=[pl.BlockSpec((tt, c), lambda i: (i, 3)),
                  pl.BlockSpec((tt, c), lambda i: (i, 4)),
                  const2((POOL_HALO, c)),
                  pl.BlockSpec(pool_w_bf.shape, lambda i: (0, 0, 0)),
                  const2((1, c))],
        out_specs=[pl.BlockSpec((tt, c), lambda i: (i, 0)), const2((POOL_HALO, c))],
        out_shape=[jax.ShapeDtypeStruct((t, c), BF16), jax.ShapeDtypeStruct((POOL_HALO, c), F32)],
        scratch_shapes=[pltpu.VMEM((tt + POOL_HALO, c), F32)],
        compiler_params=_cparams("arbitrary"),
        name="pool_seq",
    )(proj_a, proj_a, hist_p, pool_w_bf, pool_scale.reshape(1, c))
    return y, tail[POOL_HALO - POOL_HIST:]


def _pool_step_kernel(u_ref, z_ref, st_ref, pw_ref, ps_ref, y_ref, new_ref, *, pos0):
    cdim = POOL_DIM
    gd = POOL_GROUP_DIM
    u = u_ref[...]
    pos = jnp.full((u.shape[0], gd), pos0, jnp.int32)

    def window_sum(g, w):
        s = u[:, g * gd:(g + 1) * gd]
        for j in range(1, w):
            k = POOL_HIST - j
            s = s + st_ref[:, k * cdim + g * gd:k * cdim + (g + 1) * gd]
        return s

    _pool_mix(window_sum, u, z_ref[...], pos, pw_ref, ps_ref, y_ref)
    new_ref[:, 0:(POOL_HIST - 1) * cdim] = st_ref[:, cdim:POOL_HIST * cdim]
    new_ref[:, (POOL_HIST - 1) * cdim:] = u


def pool_branch_step(proj_a, state, pool_w_bf, pool_scale, pos0, bb=16):
    b = proj_a.shape[0]
    c = POOL_DIM
    bb = min(bb, b)
    st2 = state.reshape(b, POOL_HIST * c)
    y, new = pl.pallas_call(
        functools.partial(_pool_step_kernel, pos0=pos0),
        grid=(b // bb,),
        in_specs=[pl.BlockSpec((bb, c), lambda i: (i, 3)),
                  pl.BlockSpec((bb, c), lambda i: (i, 4)),
                  pl.BlockSpec((bb, POOL_HIST * c), lambda i: (i, 0)),
                  pl.BlockSpec(pool_w_bf.shape, lambda i: (0, 0, 0)),
                  pl.BlockSpec((1, c), lambda i: (0, 0))],
        out_specs=[pl.BlockSpec((bb, c), lambda i: (i, 0)),
                   pl.BlockSpec((bb, POOL_HIST * c), lambda i: (i, 0))],
        out_shape=[jax.ShapeDtypeStruct((b, c), BF16),
                   jax.ShapeDtypeStruct((b, POOL_HIST * c), F32)],
        compiler_params=_cparams("parallel"),
        name="pool_step",
    )(proj_a, proj_a, st2, pool_w_bf, pool_scale.reshape(1, c))
    return y, new.reshape(b, POOL_HIST, c)


def _rms(x, g):
    return x * lax.rsqrt(jnp.mean(x * x, axis=-1, keepdims=True) + EPS) * g


def _mla_prep_kernel(pb_ref, qg_ref, kvg_ref, ck_ref, sk_ref,
                     qn_ref, lat_ref, latb_ref, kr_ref, kadd_ref):
    q0, q1 = 0, Q_LORA_RANK
    k1 = q1 + KV_LORA_RANK
    r1 = k1 + QK_PAD
    r2 = r1 + QK_PAD
    qn_ref[...] = _rms(pb_ref[:, q0:q1], qg_ref[...]).astype(qn_ref.dtype)
    lat = _rms(pb_ref[:, q1:k1], kvg_ref[...])
    lat_ref[...] = lat
    latb_ref[...] = lat.astype(latb_ref.dtype)
    kadd = pb_ref[:, k1:r1] * ck_ref[...] + pb_ref[:, r1:r2] * sk_ref[...]
    kadd_ref[...] = kadd
    kr_ref[...] = kadd[:, ROPE_OFF:ROPE_OFF + QK_ROPE_DIM]


def mla_prep(proj_b, q_norm_g, kv_norm_g, ck, sk, tt=512):
    t = proj_b.shape[0]
    tt = min(tt, t)
    rowblk = lambda w: pl.BlockSpec((tt, w), lambda i: (i, 0))
    const = lambda w: pl.BlockSpec((1, w), lambda i: (0, 0))
    return pl.pallas_call(
        _mla_prep_kernel,
        grid=(t // tt,),
        in_specs=[rowblk(PROJ_B_DIM), const(Q_LORA_RANK), const(KV_LORA_RANK), rowblk(QK_PAD), rowblk(QK_PAD)],
        out_specs=[rowblk(Q_LORA_RANK), rowblk(KV_LORA_RANK), rowblk(KV_LORA_RANK),
                   rowblk(QK_ROPE_DIM), rowblk(QK_PAD)],
        out_shape=[jax.ShapeDtypeStruct((t, Q_LORA_RANK), BF16),
                   jax.ShapeDtypeStruct((t, KV_LORA_RANK), F32),
                   jax.ShapeDtypeStruct((t, KV_LORA_RANK), BF16),
                   jax.ShapeDtypeStruct((t, QK_ROPE_DIM), F32),
                   jax.ShapeDtypeStruct((t, QK_PAD), F32)],
        compiler_params=_cparams("parallel"),
        name="mla_prep",
    )(proj_b, q_norm_g.reshape(1, -1), kv_norm_g.reshape(1, -1), ck, sk)


def _qproj_kernel(x_ref, w1_ref, w2_ref, c_ref, s_ref, o_ref):
    x = x_ref[...]
    o_ref[0] = (_dot(x, w1_ref[0]) * c_ref[...] + _dot(x, w2_ref[0]) * s_ref[...]).astype(o_ref.dtype)


def q_proj(qn, wq1, wq2, cq, sq, bm=1024):
    t, r = qn.shape
    bm = min(bm, t)
    return pl.pallas_call(
        _qproj_kernel,
        grid=(N_HEADS, t // bm),
        in_specs=[pl.BlockSpec((bm, r), lambda h, i: (i, 0)),
                  pl.BlockSpec((1, r, QK_PAD), lambda h, i: (h, 0, 0)),
                  pl.BlockSpec((1, r, QK_PAD), lambda h, i: (h, 0, 0)),
                  pl.BlockSpec((bm, QK_PAD), lambda h, i: (i, 0)),
                  pl.BlockSpec((bm, QK_PAD), lambda h, i: (i, 0))],
        out_specs=pl.BlockSpec((1, bm, QK_PAD), lambda h, i: (h, i, 0)),
        out_shape=jax.ShapeDtypeStruct((N_HEADS, t, QK_PAD), BF16),
        compiler_params=_cparams("parallel", "parallel"),
        name="q_proj",
    )(qn, wq1, wq2, cq, sq)


def _kcat_kernel(x_ref, w_ref, add_ref, o_ref):
    o_ref[0] = (_dot(x_ref[...], w_ref[0]) + add_ref[...]).astype(o_ref.dtype)


def k_cat(latb, wuk_pad, kadd, bm=1024):
    t, r = latb.shape
    bm = min(bm, t)
    return pl.pallas_call(
        _kcat_kernel,
        grid=(t // bm, N_HEADS),
        in_specs=[pl.BlockSpec((bm, r), lambda i, h: (i, 0)),
                  pl.BlockSpec((1, r, QK_PAD), lambda i, h: (h, 0, 0)),
                  pl.BlockSpec((bm, QK_PAD), lambda i, h: (i, 0))],
        out_specs=pl.BlockSpec((1, bm, QK_PAD), lambda i, h: (h, i, 0)),
        out_shape=jax.ShapeDtypeStruct((N_HEADS, t, QK_PAD), BF16),
        compiler_params=_cparams("parallel", "parallel"),
        name="k_cat",
    )(latb, wuk_pad, kadd)


def _qproj_t_kernel(x_ref, w1_ref, w2_ref, c_ref, s_ref, o_ref):
    x = x_ref[...]
    o_ref[0] = (_dot_nt(w1_ref[0], x) * c_ref[...] + _dot_nt(w2_ref[0], x) * s_ref[...]).astype(o_ref.dtype)


def q_proj_t(qn, wq1_t, wq2_t, cq_t, sq_t, bm=1024):
    t, r = qn.shape
    bm = min(bm, t)
    return pl.pallas_call(
        _qproj_t_kernel,
        grid=(t // bm, N_HEADS),
        in_specs=[pl.BlockSpec((bm, r), lambda i, h: (i, 0)),
                  pl.BlockSpec((1, QK_PAD, r), lambda i, h: (h, 0, 0)),
                  pl.BlockSpec((1, QK_PAD, r), lambda i, h: (h, 0, 0)),
                  pl.BlockSpec((QK_PAD, bm), lambda i, h: (0, i)),
                  pl.BlockSpec((QK_PAD, bm), lambda i, h: (0, i))],
        out_specs=pl.BlockSpec((1, QK_PAD, bm), lambda i, h: (h, 0, i)),
        out_shape=jax.ShapeDtypeStruct((N_HEADS, QK_PAD, t), BF16),
        compiler_params=_cparams("parallel", "parallel"),
        name="q_proj_t",
    )(qn, wq1_t, wq2_t, cq_t, sq_t)


def _vproj_t_kernel(x_ref, w_ref, o_ref):
    o_ref[0] = _dot_nt(w_ref[0], x_ref[...]).astype(o_ref.dtype)


def v_proj_t(latb, wuv_t, bm=1024):
    t, r = latb.shape
    bm = min(bm, t)
    return pl.pallas_call(
        _vproj_t_kernel,
        grid=(t // bm, N_HEADS),
        in_specs=[pl.BlockSpec((bm, r), lambda i, h: (i, 0)),
                  pl.BlockSpec((1, V_HEAD_DIM, r), lambda i, h: (h, 0, 0))],
        out_specs=pl.BlockSpec((1, V_HEAD_DIM, bm), lambda i, h: (h, 0, i)),
        out_shape=jax.ShapeDtypeStruct((N_HEADS, V_HEAD_DIM, t), BF16),
        compiler_params=_cparams("parallel", "parallel"),
        name="v_proj_t",
    )(latb, wuv_t)


EXP2_SCALE = SOFTMAX_SCALE * 1.4426950408889634


def _flash_kernel(qt_ref, k_ref, vt_ref, z_ref, o_ref, sa_ref, sb_ref, *, blk):
    qi = pl.program_id(1)
    qt = qt_ref[0]

    def scores_into(j, s_ref):
        start = pl.multiple_of(j * blk, blk)
        s_ref[...] = _dot(k_ref[0, pl.ds(start, blk), :], qt)

    def update(j, s_ref, carry, masked=False):
        m, l, acc = carry
        s = s_ref[...]
        if masked:
            key = lax.broadcasted_iota(jnp.int32, (blk, blk), 0)
            qry = lax.broadcasted_iota(jnp.int32, (blk, blk), 1)
            s = jnp.where(key <= qry, s, -jnp.inf)
        m_new = jnp.maximum(m, jnp.max(s, axis=0, keepdims=True))
        alpha = jnp.exp2((m - m_new) * EXP2_SCALE)
        p = jnp.exp2((s - m_new) * EXP2_SCALE)
        l = l * alpha + jnp.sum(p, axis=0, keepdims=True)
        start = pl.multiple_of(j * blk, blk)
        vt = vt_ref[0, :, pl.ds(start, blk)]
        acc = acc * alpha + _dot(vt, p.astype(BF16))
        return m_new, l, acc

    def finish(carry):
        _, l, acc = carry
        o_ref[...] = ((acc / l).T * _silu(z_ref[...])).astype(o_ref.dtype)

    scores_into(0, sa_ref)

    def pair(t, carry):
        scores_into(2 * t + 1, sb_ref)
        carry = update(2 * t, sa_ref, carry)
        scores_into(2 * t + 2, sa_ref)
        return update(2 * t + 1, sb_ref, carry)

    init = (jnp.full((1, blk), NEG_INIT, F32), jnp.zeros((1, blk), F32), jnp.zeros((V_HEAD_DIM, blk), F32))
    carry = lax.fori_loop(0, qi // 2, pair, init)

    @pl.when(qi % 2 == 0)
    def _():
        finish(update(qi, sa_ref, carry, masked=True))

    @pl.when(qi % 2 == 1)
    def _():
        scores_into(qi, sb_ref)
        finish(update(qi, sb_ref, update(qi - 1, sa_ref, carry), masked=True))


def flash_attention(qt, kc, vt, proj_c, blk=512):
    t = kc.shape[1]
    blk = min(blk, t)
    return pl.pallas_call(
        functools.partial(_flash_kernel, blk=blk),
        grid=(N_HEADS, t // blk),
        in_specs=[pl.BlockSpec((1, QK_PAD, blk), lambda h, i: (h, 0, i)),
                  pl.BlockSpec((1, t, QK_PAD), lambda h, i: (h, 0, 0)),
                  pl.BlockSpec((1, V_HEAD_DIM, t), lambda h, i: (h, 0, 0)),
                  pl.BlockSpec((blk, V_HEAD_DIM), lambda h, i: (i, h))],
        out_specs=pl.BlockSpec((blk, V_HEAD_DIM), lambda h, i: (i, h)),
        out_shape=jax.ShapeDtypeStruct((t, ATTN_DIM), BF16),
        scratch_shapes=[pltpu.VMEM((blk, blk), F32), pltpu.VMEM((blk, blk), F32)],
        compiler_params=_cparams("parallel", "arbitrary"),
        name="flash_attention",
    )(qt, kc, vt, proj_c)


def _qlat_kernel(qc_ref, w_ref, o_ref):
    o_ref[0] = _dot(qc_ref[0][:, 0:QK_NOPE_DIM], w_ref[0]).astype(o_ref.dtype)


def q_latent(qc, wuk_t):
    b = qc.shape[1]
    return pl.pallas_call(
        _qlat_kernel,
        grid=(N_HEADS,),
        in_specs=[pl.BlockSpec((1, b, QK_PAD), lambda h: (h, 0, 0)),
                  pl.BlockSpec((1, QK_NOPE_DIM, KV_LORA_RANK), lambda h: (h, 0, 0))],
        out_specs=pl.BlockSpec((1, b, KV_LORA_RANK), lambda h: (h, 0, 0)),
        out_shape=jax.ShapeDtypeStruct((N_HEADS, b, KV_LORA_RANK), BF16),
        compiler_params=_cparams("parallel"),
        name="q_latent",
    )(qc, wuk_t)


def _decode_kernel(pt_ref, ql_ref, qr_ref, sl_ref, sr_ref, ckv_hbm, kr_hbm, o_ref,
                   lat_buf, rope_buf, sem, m_sc, l_sc, acc_sc, kbuf, rbuf,
                   *, layer, n_pg, n_ch, steps_per_row, total_steps):
    cp = n_pg // n_ch
    s_cur = pl.program_id(0)
    last = total_steps - 1
    ql = ql_ref[0]
    qr = qr_ref[0]

    def page_copies(step, slot):
        row = step // steps_per_row
        first = (step % steps_per_row) * n_pg
        copies = []
        for p in range(n_pg):
            page = pt_ref[row, first + p]
            copies.append(pltpu.make_async_copy(ckv_hbm.at[layer, page], lat_buf.at[slot, p], sem.at[slot, 0]))
            copies.append(pltpu.make_async_copy(kr_hbm.at[layer, page], rope_buf.at[slot, p], sem.at[slot, 1]))
        return copies

    def start(step, slot):
        for cpy in page_copies(step, slot):
            cpy.start()

    def wait(step, slot):
        for cpy in page_copies(step, slot):
            cpy.wait()

    def compute(slot, j):
        @pl.when(j == 0)
        def _():
            sl = sl_ref[0].astype(BF16).astype(F32)
            sr = sr_ref[0].astype(BF16).astype(F32)
            s0 = (jnp.sum(ql.astype(F32) * sl, axis=-1, keepdims=True)
                  + jnp.sum(qr.astype(F32) * sr, axis=-1, keepdims=True)) * SOFTMAX_SCALE
            m_sc[...] = jnp.maximum(s0, NEG_INIT)
            l_sc[...] = jnp.ones_like(l_sc)
            acc_sc[...] = jnp.broadcast_to(sl, acc_sc.shape)

        parts = []
        for c in range(n_ch):
            for p in range(cp):
                kbuf[c, p * PAGE_SIZE:(p + 1) * PAGE_SIZE, :] = lat_buf[slot, c * cp + p].astype(BF16)
                rbuf[c, :, p * PAGE_SIZE:(p + 1) * PAGE_SIZE] = rope_buf[slot, c * cp + p].astype(BF16)
            k = kbuf[c]
            s = (_dot_nt(ql, k) + _dot(qr, rbuf[c])) * SOFTMAX_SCALE
            m_c = jnp.max(s, axis=-1, keepdims=True)
            p_c = jnp.exp(s - m_c)
            parts.append((m_c, jnp.sum(p_c, axis=-1, keepdims=True), _dot(p_c.astype(BF16), k)))
        m_prev = m_sc[...]
        m_new = m_prev
        for m_c, _, _ in parts:
            m_new = jnp.maximum(m_new, m_c)
        alpha = jnp.exp(m_prev - m_new)
        l = l_sc[...] * alpha
        acc = acc_sc[...] * alpha
        for m_c, l_c, acc_c in parts:
            w = jnp.exp(m_c - m_new)
            l = l + l_c * w
            acc = acc + acc_c * w
        l_sc[...] = l
        acc_sc[...] = acc
        m_sc[...] = m_new

    slot_of = lambda step: step % DECODE_SLOTS
    j = s_cur % steps_per_row

    @pl.when(s_cur == 0)
    def _():
        start(0, 0)
        start(jnp.minimum(1, last), 1)

    ahead = jnp.minimum(s_cur + 2, last)
    start(ahead, slot_of(s_cur + 2))
    wait(s_cur, slot_of(s_cur))
    compute(slot_of(s_cur), j)

    @pl.when(j == steps_per_row - 1)
    def _():
        o_ref[0] = acc_sc[...] / l_sc[...]

    @pl.when(s_cur == last)
    def _():
        wait(last, slot_of(last + 1))
        wait(last, slot_of(last + 2))


def decode_attention(q_lat, q_rope, lat_new, kr_new, ckv_cache, kr_cache_t, page_table, layer):
    b, n_pages = page_table.shape
    n_pg = next(n for n in (DECODE_PAGES_PER_STEP, 4, 1) if n_pages % n == 0)
    n_ch = min(DECODE_CHUNKS_PER_STEP, n_pg)
    cp = n_pg // n_ch
    steps_per_row = n_pages // n_pg

    per_row = lambda d1, d2: pl.BlockSpec((1, d1, d2), lambda s, pt: (s // steps_per_row, 0, 0))
    hbm = pl.BlockSpec(memory_space=pl.ANY)
    grid_spec = pltpu.PrefetchScalarGridSpec(
        num_scalar_prefetch=1,
        grid=(b * steps_per_row,),
        in_specs=[per_row(N_HEADS, KV_LORA_RANK), per_row(N_HEADS, QK_ROPE_DIM),
                  per_row(1, KV_LORA_RANK), per_row(1, QK_ROPE_DIM), hbm, hbm],
        out_specs=per_row(N_HEADS, KV_LORA_RANK),
        scratch_shapes=[pltpu.VMEM((DECODE_SLOTS, n_pg, PAGE_SIZE, KV_LORA_RANK), F32),
                        pltpu.VMEM((DECODE_SLOTS, n_pg, QK_ROPE_DIM, PAGE_SIZE), F32),
                        pltpu.SemaphoreType.DMA((DECODE_SLOTS, 2)),
                        pltpu.VMEM((N_HEADS, 1), F32), pltpu.VMEM((N_HEADS, 1), F32),
                        pltpu.VMEM((N_HEADS, KV_LORA_RANK), F32),
                        pltpu.VMEM((n_ch, cp * PAGE_SIZE, KV_LORA_RANK), BF16),
                        pltpu.VMEM((n_ch, QK_ROPE_DIM, cp * PAGE_SIZE), BF16)],
    )
    return pl.pallas_call(
        functools.partial(_decode_kernel, layer=layer, n_pg=n_pg, n_ch=n_ch,
                          steps_per_row=steps_per_row, total_steps=b * steps_per_row),
        grid_spec=grid_spec,
        out_shape=jax.ShapeDtypeStruct((b, N_HEADS, KV_LORA_RANK), F32),
        compiler_params=_cparams("arbitrary"),
        name="decode_attention",
    )(page_table, q_lat, q_rope, lat_new.reshape(b, 1, -1), kr_new.reshape(b, 1, -1),
      ckv_cache, kr_cache_t)


def _oproj_kernel(ol_ref, w_ref, z_ref, o_ref):
    o = _dot(ol_ref[0].astype(BF16), w_ref[0])
    o_ref[...] = (o * _silu(z_ref[...])).astype(o_ref.dtype)


def o_proj(o_lat_h, wuv_h, proj_c):
    b = o_lat_h.shape[1]
    return pl.pallas_call(
        _oproj_kernel,
        grid=(N_HEADS,),
        in_specs=[pl.BlockSpec((1, b, KV_LORA_RANK), lambda h: (h, 0, 0)),
                  pl.BlockSpec((1, KV_LORA_RANK, V_HEAD_DIM), lambda h: (h, 0, 0)),
                  pl.BlockSpec((b, V_HEAD_DIM), lambda h: (0, h))],
        out_specs=pl.BlockSpec((b, V_HEAD_DIM), lambda h: (0, h)),
        out_shape=jax.ShapeDtypeStruct((b, ATTN_DIM), BF16),
        compiler_params=_cparams("parallel"),
        name="o_proj",
    )(o_lat_h, wuv_h, proj_c)


def _merge_kernel(yc_ref, yp_ref, ya_ref, wc_ref, wp_ref, wa_ref, g0_ref, g1_ref, g2_ref, o_ref):
    merged = (_sigmoid(g0_ref[...]) * _dot(yc_ref[...], wc_ref[...])
              + _sigmoid(g1_ref[...]) * _dot(yp_ref[...], wp_ref[...])
              + _sigmoid(g2_ref[...]) * _dot(ya_ref[...], wa_ref[...]))
    o_ref[...] = merged.astype(o_ref.dtype)


def gated_merge(yc, yp, ya, wc, wp, wa, proj_c, bm=1024, bn=512):
    t = yc.shape[0]
    d = D_MODEL
    bm = min(bm, t)
    gate_blk0 = ATTN_DIM // bn
    per_branch = d // bn
    act = lambda w: pl.BlockSpec((bm, w), lambda j, i: (i, 0))
    wsp = lambda k: pl.BlockSpec((k, bn), lambda j, i: (0, j))
    gate = lambda br: pl.BlockSpec((bm, bn), lambda j, i: (i, gate_blk0 + br * per_branch + j))
    return pl.pallas_call(
        _merge_kernel,
        grid=(d // bn, t // bm),
        in_specs=[act(CONV_DIM), act(POOL_DIM), act(ATTN_DIM),
                  wsp(CONV_DIM), wsp(POOL_DIM), wsp(ATTN_DIM),
                  gate(0), gate(1), gate(2)],
        out_specs=pl.BlockSpec((bm, bn), lambda j, i: (i, j)),
        out_shape=jax.ShapeDtypeStruct((t, d), BF16),
        compiler_params=_cparams("parallel", "parallel"),
        name="gated_merge",
    )(yc, yp, ya, wc, wp, wa, proj_c, proj_c, proj_c)


def _out_kernel(m_ref, w_ref, x_ref, o_ref):
    o_ref[...] = x_ref[...] + _dot(m_ref[...], w_ref[...])


def out_proj_residual(merged, w_out, x, bm=1024, bn=1024):
    t, d = x.shape
    bm = min(bm, t)
    return pl.pallas_call(
        _out_kernel,
        grid=(d // bn, t // bm),
        in_specs=[pl.BlockSpec((bm, d), lambda j, i: (i, 0)),
                  pl.BlockSpec((d, bn), lambda j, i: (0, j)),
                  pl.BlockSpec((bm, bn), lambda j, i: (i, j))],
        out_specs=pl.BlockSpec((bm, bn), lambda j, i: (i, j)),
        out_shape=jax.ShapeDtypeStruct((t, d), F32),
        compiler_params=_cparams("parallel", "parallel"),
        name="out_proj",
    )(merged, w_out, x)


def _swap_halves(w):
    half = QK_ROPE_DIM // 2
    return jnp.concatenate([w[..., half:], w[..., :half]], axis=-1)


def _prep_layer_weights(w_in, w_q_up, w_uk, w_uv, pool_w, w_conv_out, w_pool_out, w_attn_out, w_out):
    d = w_in.shape[0]
    kr = w_in[:, OFF_KR:OFF_KR + QK_ROPE_DIM]
    zl = jnp.zeros((d, ROPE_OFF), F32)
    zr = jnp.zeros((d, QK_PAD - ROPE_OFF - QK_ROPE_DIM), F32)
    w_b = jnp.concatenate([w_in[:, OFF_CQ:OFF_KR], zl, kr, zr, zl, _swap_halves(kr), zr], axis=1)
    r = w_q_up.shape[0]
    wq = w_q_up.reshape(r, N_HEADS, QK_NOPE_DIM + QK_ROPE_DIM).transpose(1, 0, 2)
    wq_nope, wq_rope = wq[..., :QK_NOPE_DIM], wq[..., QK_NOPE_DIM:]
    zq = jnp.zeros((N_HEADS, r, QK_PAD - ROPE_OFF - QK_ROPE_DIM), F32)
    wq1 = jnp.concatenate([wq_nope, wq_rope, zq], axis=-1)
    wq2 = jnp.concatenate([jnp.zeros_like(wq_nope), _swap_halves(wq_rope), zq], axis=-1)
    wuk_h = w_uk.transpose(1, 0, 2)
    wuk_pad = jnp.concatenate([wuk_h, jnp.zeros((N_HEADS, KV_LORA_RANK, QK_PAD - QK_NOPE_DIM), F32)], axis=-1)
    bf = lambda a: a.astype(BF16)
    return dict(
        w_a=bf(w_in[:, :PROJ_A_DIM]), w_b=bf(w_b), w_c=bf(w_in[:, OFF_ZATTN:]),
        wq1=bf(wq1), wq2=bf(wq2), wuk_pad=bf(wuk_pad),
        wq1_t=bf(wq1.transpose(0, 2, 1)), wq2_t=bf(wq2.transpose(0, 2, 1)),
        wuk_t=bf(w_uk.transpose(1, 2, 0)),
        wuv_t=bf(w_uv.transpose(1, 2, 0)),
        wuv_h=bf(w_uv.transpose(1, 0, 2)),
        pool_w=bf(pool_w), w_conv_out=bf(w_conv_out), w_pool_out=bf(w_pool_out),
        w_attn_out=bf(w_attn_out), w_out=bf(w_out),
    )


def _rope_tables(pos):
    half = QK_ROPE_DIM // 2
    freqs = ROPE_THETA ** (-jnp.arange(half, dtype=F32) / half)
    ang = pos.astype(F32)[:, None] * freqs[None, :]
    cos, sin = jnp.cos(ang), jnp.sin(ang)
    t = pos.shape[0]
    zl = jnp.zeros((t, ROPE_OFF), F32)
    zr = jnp.zeros((t, QK_PAD - ROPE_OFF - QK_ROPE_DIM), F32)
    ck = jnp.concatenate([zl, cos, cos, zr], axis=1)
    sk = jnp.concatenate([zl, -sin, sin, zr], axis=1)
    cq = jnp.concatenate([jnp.ones((t, ROPE_OFF), F32), cos, cos, zr], axis=1)
    return ck, sk, cq


def _layer_common(x, wts, norm_g, q_norm_g, kv_norm_g, tables):
    ck, sk, cq = tables
    h = rmsnorm(x, norm_g, BF16)
    proj_a = matmul(h, wts["w_a"], F32, name="proj_a")
    proj_b = matmul(h, wts["w_b"], F32, bn=PROJ_B_DIM, name="proj_b")
    proj_c = matmul(h, wts["w_c"], F32, name="proj_c")
    qn, lat, latb, kr, kadd = mla_prep(proj_b, q_norm_g, kv_norm_g, ck, sk)
    return proj_a, proj_c, qn, lat, latb, kr, kadd


def _layer_finish(x, y_conv, y_pool, y_attn, proj_c, wts):
    merged = gated_merge(y_conv, y_pool, y_attn, wts["w_conv_out"], wts["w_pool_out"], wts["w_attn_out"], proj_c)
    return out_proj_residual(merged, wts["w_out"], x)


def _prompt_layer(x, wts, p, tables):
    proj_a, proj_c, qn, lat, latb, kr, kadd = _layer_common(x, wts, p["norm_g"], p["q_norm_g"], p["kv_norm_g"], tables)
    y_conv, new_conv = conv_branch_seq(proj_a, jnp.zeros((CONV_HIST, CONV_DIM), F32),
                                       p["conv_w"], p["conv_b"], p["conv_ln_g"], p["conv_ln_b"])
    y_pool, new_pool = pool_branch_seq(proj_a, jnp.zeros((POOL_HIST, POOL_DIM), F32),
                                       wts["pool_w"], p["pool_scale"], 0)
    _, sk, cq = tables
    qt = q_proj_t(qn, wts["wq1_t"], wts["wq2_t"], cq.T, sk.T)
    kc = k_cat(latb, wts["wuk_pad"], kadd)
    vt = v_proj_t(latb, wts["wuv_t"])
    y_attn = flash_attention(qt, kc, vt, proj_c)
    x_new = _layer_finish(x, y_conv, y_pool, y_attn, proj_c, wts)
    return x_new, new_conv, new_pool, lat, kr


def _sample_layer(x, wts, p, tables, state_conv, state_pool, ckv_cache, kr_cache_t, page_table, layer):
    proj_a, proj_c, qn, lat, latb, kr, kadd = _layer_common(x, wts, p["norm_g"], p["q_norm_g"], p["kv_norm_g"], tables)
    y_conv, new_conv = conv_branch_step(proj_a, state_conv, p["conv_w"], p["conv_b"], p["conv_ln_g"], p["conv_ln_b"])
    y_pool, new_pool = pool_branch_step(proj_a, state_pool, wts["pool_w"], p["pool_scale"], PAST_LEN)
    _, sk, cq = tables
    qc = q_proj(qn, wts["wq1"], wts["wq2"], cq, sk)
    q_lat = q_latent(qc, wts["wuk_t"]).transpose(1, 0, 2)
    q_rope = qc[:, :, ROPE_OFF:ROPE_OFF + QK_ROPE_DIM].transpose(1, 0, 2)
    o_lat = decode_attention(q_lat, q_rope, lat, kr, ckv_cache, kr_cache_t, page_table, layer)
    y_attn = o_proj(o_lat.transpose(1, 0, 2), wts["wuv_h"], proj_c)
    x_new = _layer_finish(x, y_conv, y_pool, y_attn, proj_c, wts)
    return x_new, new_conv, new_pool, lat, kr


def kernel(x_prompt, x_sample, state_conv, state_pool, cache_kv_latent, cache_k_rope, page_table, norm_g, w_in, conv_w, conv_b, conv_ln_g, conv_ln_b, w_conv_out, pool_w, pool_scale, w_pool_out, q_norm_g, w_q_up, kv_norm_g, w_uk, w_uv, w_attn_out, w_out, final_norm_g):
    n_p, t_p, d = x_prompt.shape
    n_s, t_s, _ = x_sample.shape
    assert n_p == 1 and t_s == 1
    depth = w_in.shape[0]
    tables_p = _rope_tables(jnp.arange(t_p, dtype=jnp.int32))
    tables_s = _rope_tables(jnp.full((n_s,), PAST_LEN, jnp.int32))
    yp = x_prompt.reshape(t_p, d)
    ys = x_sample.reshape(n_s, d)
    kr_cache_t = jnp.swapaxes(cache_k_rope, 2, 3)
    outs_p, outs_s = [], []
    for l in range(depth):
        wts = _prep_layer_weights(w_in[l], w_q_up[l], w_uk[l], w_uv[l], pool_w[l],
                                  w_conv_out[l], w_pool_out[l], w_attn_out[l], w_out[l])
        p = dict(norm_g=norm_g[l], q_norm_g=q_norm_g[l], kv_norm_g=kv_norm_g[l],
                 conv_w=conv_w[l], conv_b=conv_b[l], conv_ln_g=conv_ln_g[l], conv_ln_b=conv_ln_b[l],
                 pool_scale=pool_scale[l])
        ys, *rest_s = _sample_layer(ys, wts, p, tables_s, state_conv[l], state_pool[l],
                                    cache_kv_latent, kr_cache_t, page_table, l)
        yp, *rest_p = _prompt_layer(yp, wts, p, tables_p)
        outs_p.append(rest_p)
        outs_s.append(rest_s)
    y_prompt = rmsnorm(yp, final_norm_g, F32).reshape(n_p, t_p, d)
    y_sample = rmsnorm(ys, final_norm_g, F32).reshape(n_s, t_s, d)
    stack_p = lambda k: jnp.stack([o[k] for o in outs_p])[:, None]
    stack_s = lambda k: jnp.stack([o[k] for o in outs_s])
    return (y_prompt, y_sample,
            stack_p(0), stack_p(1), stack_p(2), stack_p(3),
            stack_s(0), stack_s(1), stack_s(2)[:, :, None, :], stack_s(3)[:, :, None, :])
```
